```python
import math
import jax, jax.numpy as jnp
from jax import lax
import numpy as np

D_MODEL = 2048
BATCH = 2
SEQ = 4096
DEPTH = 1
DEC_BATCH = 128
DEC_SEQ = 4
PAST_LEN = 16384
PAGE_SIZE = 128

MLA_HEADS = 8
MLA_Q_LORA = 512
MLA_KV_LORA = 512
MLA_NOPE = 128
MLA_ROPE = 64
MLA_V = 128
ROPE_THETA = 10000.0
MLA_Q_BLOCK = 128
MOBA_HEADS = 8
MOBA_KV_HEADS = 2
MOBA_HEAD_DIM = 128
MOBA_BLOCK = 256
MOBA_TOPK = 3
MOBA_Q_BLOCK = 64
REL_BUCKETS = 32
REL_MAX_DIST = 128
PEER_HEADS = 8
PEER_NKEYS = 128
PEER_N_EXPERTS = PEER_NKEYS * PEER_NKEYS
PEER_QDIM = 256
PEER_TOPK = 16
PEER_ROW_BLOCK = 128
PLE_DIM = 256
NORM_EPS = 1e-6
NEG_INF = -1e30
IN_WIDTHS = (MLA_Q_LORA, MLA_KV_LORA, MLA_ROPE, MOBA_HEADS * MOBA_HEAD_DIM,
             MOBA_KV_HEADS * MOBA_HEAD_DIM, MOBA_KV_HEADS * MOBA_HEAD_DIM, D_MODEL, D_MODEL)
IN_DIM = sum(IN_WIDTHS)

kernel_name = 'hybrid_mla_moba_peer_step'


def rms_norm(x, g):
    xf = x.astype(jnp.float32)
    y = xf * lax.rsqrt(jnp.mean(xf * xf, axis=-1, keepdims=True) + NORM_EPS)
    return (y * g.astype(jnp.float32)).astype(x.dtype)


def rope_angles(pos, dtype):
    inv = jnp.exp(-math.log(ROPE_THETA) * jnp.arange(0, MLA_ROPE, 2, dtype=jnp.float32) / MLA_ROPE)
    ang = pos.astype(jnp.float32)[:, None] * inv[None, :]
    return jnp.cos(ang).astype(dtype), jnp.sin(ang).astype(dtype)


def apply_rope(x, cos, sin):
    x1, x2 = jnp.split(x, 2, axis=-1)
    return jnp.concatenate([x1 * cos - x2 * sin, x2 * cos + x1 * sin], axis=-1)


def t5_bucket(dist):
    dist = jnp.maximum(dist, 0)
    max_exact = REL_BUCKETS // 2
    scaled = (jnp.log(jnp.maximum(dist, 1).astype(jnp.float32) / max_exact)
              / math.log(REL_MAX_DIST / max_exact) * (REL_BUCKETS - max_exact))
    large = jnp.minimum(max_exact + scaled.astype(jnp.int32), REL_BUCKETS - 1)
    return jnp.where(dist < max_exact, dist, large)


def kv_of_head():
    return jnp.arange(MOBA_HEADS) // (MOBA_HEADS // MOBA_KV_HEADS)


def block_means(kb):
    return jnp.mean(kb.astype(jnp.float32), axis=1).astype(kb.dtype)


def mixer_inputs(h, pos, w_in, g_q_lat, w_uq, g_kv_lat, w_uk):
    z = h @ w_in
    offs = [int(o) for o in np.cumsum(IN_WIDTHS)[:-1]]
    zq, zc, zr, zmq, zmk, zmv, za, zb = jnp.split(z, offs, axis=-1)
    lead = h.shape[:-1]
    q = (rms_norm(zq, g_q_lat) @ w_uq).reshape(lead + (MLA_HEADS, MLA_NOPE + MLA_ROPE))
    q_nope, q_rope = q[..., :MLA_NOPE], q[..., MLA_NOPE:]
    cos, sin = rope_angles(pos, h.dtype)
    q_rope = apply_rope(q_rope, cos[:, None, :], sin[:, None, :])
    k_rope = apply_rope(zr, cos, sin)
    c_kv = rms_norm(zc, g_kv_lat)
    q_abs = jnp.einsum('...hn,chn->...hc', q_nope, w_uk)
    mq = zmq.reshape(lead + (MOBA_HEADS, MOBA_HEAD_DIM))
    mk = zmk.reshape(lead + (MOBA_KV_HEADS, MOBA_HEAD_DIM))
    mv = zmv.reshape(lead + (MOBA_KV_HEADS, MOBA_HEAD_DIM))
    return q_abs, q_rope, c_kv, k_rope, mq, mk, mv, za, zb


def mla_core(q_abs, q_rope, c, kr, valid):
    scale = (MLA_NOPE + MLA_ROPE) ** -0.5
    s = (jnp.einsum('qhc,kc->qhk', q_abs, c) + jnp.einsum('qhr,kr->qhk', q_rope, kr)).astype(jnp.float32) * scale
    s = jnp.where(valid[:, None, :], s, NEG_INF)
    p = jax.nn.softmax(s, axis=-1).astype(c.dtype)
    return jnp.einsum('qhk,kc->qhc', p, c)


def mla_prompt(q_abs, q_rope, c_kv, k_rope):
    seq = q_abs.shape[1]
    kpos = jnp.arange(seq)

    def chunk(ci):
        start = ci * MLA_Q_BLOCK
        qpos = start + jnp.arange(MLA_Q_BLOCK)
        qa = lax.dynamic_slice_in_dim(q_abs, start, MLA_Q_BLOCK, axis=1)
        qr = lax.dynamic_slice_in_dim(q_rope, start, MLA_Q_BLOCK, axis=1)
        valid = kpos[None, :] <= qpos[:, None]
        return jax.vmap(mla_core, in_axes=(0, 0, 0, 0, None))(qa, qr, c_kv, k_rope, valid)

    o = lax.map(chunk, jnp.arange(seq // MLA_Q_BLOCK))
    return jnp.moveaxis(o, 0, 1).reshape(q_abs.shape)


def moba_select(kb, vb, idx, n_past):
    nq, nh, nk = idx.shape
    kvh = kv_of_head()[None, :, None]
    k_sel = kb[idx, :, kvh, :].reshape(nq, nh, nk * MOBA_BLOCK, MOBA_HEAD_DIM)
    v_sel = vb[idx, :, kvh, :].reshape(nq, nh, nk * MOBA_BLOCK, MOBA_HEAD_DIM)
    pos = (idx[..., None] * MOBA_BLOCK + jnp.arange(MOBA_BLOCK)).reshape(nq, nh, nk * MOBA_BLOCK)
    valid = jnp.broadcast_to((idx < n_past)[..., None], idx.shape + (MOBA_BLOCK,)).reshape(nq, nh, nk * MOBA_BLOCK)
    return k_sel, v_sel, pos, valid


def moba_core(q, qpos, k_own, v_own, own_pos, rel_bias, sel):
    scale = MOBA_HEAD_DIM ** -0.5
    kvh = kv_of_head()
    k_own_h, v_own_h = k_own[:, kvh], v_own[:, kvh]
    d_own = qpos[:, None] - own_pos[None, :]
    lo = (jnp.einsum('qhd,lhd->qhl', q, k_own_h).astype(jnp.float32) * scale
          + jnp.transpose(rel_bias[t5_bucket(d_own)], (0, 2, 1)).astype(jnp.float32))
    lo = jnp.where((d_own >= 0)[:, None, :], lo, NEG_INF)
    if sel is None:
        p = jax.nn.softmax(lo, axis=-1).astype(v_own.dtype)
        return jnp.einsum('qhl,lhd->qhd', p, v_own_h)
    k_sel, v_sel, sel_pos, sel_valid = sel
    heads = jnp.arange(MOBA_HEADS)[None, :, None]
    ls = (jnp.einsum('qhd,qhmd->qhm', q, k_sel).astype(jnp.float32) * scale
          + rel_bias[t5_bucket(qpos[:, None, None] - sel_pos), heads].astype(jnp.float32))
    ls = jnp.where(sel_valid, ls, NEG_INF)
    m = ls.shape[-1]
    p = jax.nn.softmax(jnp.concatenate([ls, lo], axis=-1), axis=-1).astype(v_own.dtype)
    return jnp.einsum('qhm,qhmd->qhd', p[..., :m], v_sel) + jnp.einsum('qhl,lhd->qhd', p[..., m:], v_own_h)


def moba_prompt(mq, mk, mv, rel_bias):
    nb_, seq = mq.shape[:2]
    n_blocks = -(-seq // MOBA_BLOCK)
    pad = n_blocks * MOBA_BLOCK - seq
    kpad = jnp.pad(mk, ((0, 0), (0, pad), (0, 0), (0, 0)))
    vpad = jnp.pad(mv, ((0, 0), (0, pad), (0, 0), (0, 0)))
    kb = kpad.reshape(nb_, n_blocks, MOBA_BLOCK, MOBA_KV_HEADS, MOBA_HEAD_DIM)
    vb = vpad.reshape(nb_, n_blocks, MOBA_BLOCK, MOBA_KV_HEADS, MOBA_HEAD_DIM)
    n_sel = min(MOBA_TOPK, n_blocks - 1)
    if n_sel > 0:
        kmean = jax.vmap(block_means)(kb)[:, :, kv_of_head()]
        scores = jnp.einsum('bshd,bnhd->bshn', mq, kmean).astype(jnp.float32)
        past_ok = jnp.arange(n_blocks)[None, :] < (jnp.arange(seq) // MOBA_BLOCK)[:, None]
        scores = jnp.where(past_ok[None, :, None, :], scores, NEG_INF)
        _, idx = lax.top_k(scores, n_sel)

    def chunk(ci):
        start = ci * MOBA_Q_BLOCK
        qpos = start + jnp.arange(MOBA_Q_BLOCK)
        ob = start // MOBA_BLOCK
        q_c = lax.dynamic_slice_in_dim(mq, start, MOBA_Q_BLOCK, axis=1)
        k_own = lax.dynamic_slice_in_dim(kpad, ob * MOBA_BLOCK, MOBA_BLOCK, axis=1)
        v_own = lax.dynamic_slice_in_dim(vpad, ob * MOBA_BLOCK, MOBA_BLOCK, axis=1)
        own_pos = ob * MOBA_BLOCK + jnp.arange(MOBA_BLOCK)
        if n_sel > 0:
            idx_c = lax.dynamic_slice_in_dim(idx, start, MOBA_Q_BLOCK, axis=1)

            def per_seq(q, kb_s, vb_s, ko, vo, ix):
                return moba_core(q, qpos, ko, vo, own_pos, rel_bias, moba_select(kb_s, vb_s, ix, ob))
            return jax.vmap(per_seq)(q_c, kb, vb, k_own, v_own, idx_c)
        return jax.vmap(lambda q, ko, vo: moba_core(q, qpos, ko, vo, own_pos, rel_bias, None))(q_c, k_own, v_own)

    o = lax.map(chunk, jnp.arange(seq // MOBA_Q_BLOCK))
    return jnp.moveaxis(o, 0, 1).reshape(mq.shape)


def sample_attention(layer, page_table, q_abs, q_rope, c_kv, k_rope, mq, mk, mv,
                     cache_ckv, cache_krope, cache_k, cache_v, rel_bias):
    past_len = page_table.shape[1] * cache_ckv.shape[2]
    ds = q_abs.shape[1]
    n_full = past_len // MOBA_BLOCK
    tail = n_full * MOBA_BLOCK
    n_sel = min(MOBA_TOPK, n_full)
    kvh = kv_of_head()
    qpos = past_len + jnp.arange(ds)
    own_pos = jnp.concatenate([jnp.arange(tail, past_len), qpos])
    mla_valid = jnp.concatenate([jnp.ones((ds, past_len), bool), jnp.tril(jnp.ones((ds, ds), bool))], axis=1)

    def per_seq(args):
        pages, qa, qr, c_new, kr_new, q_b, k_new, v_new = args
        c_all = jnp.concatenate([cache_ckv[layer, pages].reshape(past_len, MLA_KV_LORA), c_new], axis=0)
        kr_all = jnp.concatenate([cache_krope[layer, pages].reshape(past_len, MLA_ROPE), kr_new], axis=0)
        o_lat = mla_core(qa, qr, c_all, kr_all, mla_valid)
        kp = cache_k[layer, pages].reshape(past_len, MOBA_KV_HEADS, MOBA_HEAD_DIM)
        vp = cache_v[layer, pages].reshape(past_len, MOBA_KV_HEADS, MOBA_HEAD_DIM)
        k_own = jnp.concatenate([kp[tail:], k_new], axis=0)
        v_own = jnp.concatenate([vp[tail:], v_new], axis=0)
        sel = None
        if n_sel > 0:
            kb = kp[:tail].reshape(n_full, MOBA_BLOCK, MOBA_KV_HEADS, MOBA_HEAD_DIM)
            vb = vp[:tail].reshape(n_full, MOBA_BLOCK, MOBA_KV_HEADS, MOBA_HEAD_DIM)
            scores = jnp.einsum('qhd,nhd->qhn', q_b, block_means(kb)[:, kvh]).astype(jnp.float32)
            _, idx = lax.top_k(scores, n_sel)
            sel = moba_select(kb, vb, idx, n_full)
        o_moba = moba_core(q_b, qpos, k_own, v_own, own_pos, rel_bias, sel)
        return o_lat, o_moba

    return lax.map(per_seq, (page_table, q_abs, q_rope, c_kv, k_rope, mq, mk, mv))


def mixer_merge(x, o_lat, o_moba, za, zb, w_uv, w_a_out, w_b_out, w_o):
    lead = x.shape[:-1]
    a = jnp.einsum('...hc,chv->...hv', o_lat, w_uv).reshape(lead + (MLA_HEADS * MLA_V,)) @ w_a_out
    b = o_moba.reshape(lead + (MOBA_HEADS * MOBA_HEAD_DIM,)) @ w_b_out
    return x + (jax.nn.sigmoid(za) * a + jax.nn.sigmoid(zb) * b) @ w_o


def peer_block(h, wq, sub_keys, u_tab, v_tab):
    rows = h.shape[0]
    q = (h @ wq).reshape(rows, PEER_HEADS, 2, PEER_QDIM // 2)
    s = jnp.einsum('tpid,pind->tpin', q, sub_keys).astype(jnp.float32)
    s_top, i_top = lax.top_k(s, PEER_TOPK)
    cand = (s_top[:, :, 0, :, None] + s_top[:, :, 1, None, :]).reshape(rows, PEER_HEADS, PEER_TOPK * PEER_TOPK)
    cidx = (i_top[:, :, 0, :, None] * PEER_NKEYS + i_top[:, :, 1, None, :]).reshape(rows, PEER_HEADS, PEER_TOPK * PEER_TOPK)
    f_s, f_i = lax.top_k(cand, PEER_TOPK)
    expert = jnp.take_along_axis(cidx, f_i, axis=-1)
    gate = jax.nn.softmax(f_s, axis=-1)
    act = jax.nn.gelu(jnp.einsum('td,tpkd->tpk', h, u_tab[expert]).astype(jnp.float32), approximate=False)
    return jnp.einsum('tpk,tpkd->td', (gate * act).astype(h.dtype), v_tab[expert])


def peer_ffn(h, wq, sub_keys, u_tab, v_tab):
    flat = h.reshape(-1, h.shape[-1])
    n = flat.shape[0]
    n_blk = -(-n // PEER_ROW_BLOCK)
    flat = jnp.pad(flat, ((0, n_blk * PEER_ROW_BLOCK - n), (0, 0)))
    out = lax.map(lambda hb: peer_block(hb, wq, sub_keys, u_tab, v_tab),
                  flat.reshape(n_blk, PEER_ROW_BLOCK, flat.shape[-1]))
    return out.reshape(n_blk * PEER_ROW_BLOCK, -1)[:n].reshape(h.shape)


def channel_and_ple(x, p, g_ffn, peer_wq, peer_subkeys, peer_u, peer_v, g_ple, w_ple_gate, w_ple_proj):
    x = x + peer_ffn(rms_norm(x, g_ffn), peer_wq, peer_subkeys, peer_u, peer_v)
    return x + jax.nn.sigmoid(rms_norm(x, g_ple) @ w_ple_gate) * (p @ w_ple_proj)


def setup_inputs(seed: int = 0) -> dict:
    key = jax.random.key(seed)
    ks = jax.random.split(key, 32)
    f32 = jnp.float32

    def nrm(k, shape, scale):
        return jax.random.normal(k, shape, f32) * scale

    def gain(k, shape):
        return 1.0 + 0.1 * jax.random.normal(k, shape, f32)

    n_pages = PAST_LEN // PAGE_SIZE
    n_used = DEC_BATCH * n_pages
    n_pool = n_used + max(1, n_used // 4)
    page_table = jax.random.permutation(ks[0], n_pool)[:n_used].reshape(DEC_BATCH, n_pages).astype(jnp.int32)
    return {
        'x_prompt': nrm(ks[1], (BATCH, SEQ, D_MODEL), 1.0),
        'x_sample': nrm(ks[2], (DEC_BATCH, DEC_SEQ, D_MODEL), 1.0),
        'cache_mla_ckv': nrm(ks[3], (DEPTH, n_pool, PAGE_SIZE, MLA_KV_LORA), 1.0),
        'cache_mla_krope': nrm(ks[4], (DEPTH, n_pool, PAGE_SIZE, MLA_ROPE), 1.0),
        'cache_moba_k': nrm(ks[5], (DEPTH, n_pool, PAGE_SIZE, MOBA_KV_HEADS, MOBA_HEAD_DIM), 1.0),
        'cache_moba_v': nrm(ks[6], (DEPTH, n_pool, PAGE_SIZE, MOBA_KV_HEADS, MOBA_HEAD_DIM), 1.0),
        'page_table': page_table,
        'p_prompt': nrm(ks[7], (DEPTH, BATCH, SEQ, PLE_DIM), 1.0),
        'p_sample': nrm(ks[8], (DEPTH, DEC_BATCH, DEC_SEQ, PLE_DIM), 1.0),
        'rel_bias': nrm(ks[9], (REL_BUCKETS, MOBA_HEADS), 0.5),
        'g_mix': gain(ks[10], (DEPTH, D_MODEL)),
        'w_in': nrm(ks[11], (DEPTH, D_MODEL, IN_DIM), D_MODEL ** -0.5),
        'g_q_lat': gain(ks[12], (DEPTH, MLA_Q_LORA)),
        'w_uq': nrm(ks[13], (DEPTH, MLA_Q_LORA, MLA_HEADS * (MLA_NOPE + MLA_ROPE)), MLA_Q_LORA ** -0.5),
        'g_kv_lat': gain(ks[14], (DEPTH, MLA_KV_LORA)),
        'w_uk': nrm(ks[15], (DEPTH, MLA_KV_LORA, MLA_HEADS, MLA_NOPE), MLA_KV_LORA ** -0.5),
        'w_uv': nrm(ks[16], (DEPTH, MLA_KV_LORA, MLA_HEADS, MLA_V), MLA_KV_LORA ** -0.5),
        'w_a_out': nrm(ks[17], (DEPTH, MLA_HEADS * MLA_V, D_MODEL), (MLA_HEADS * MLA_V) ** -0.5),
        'w_b_out': nrm(ks[18], (DEPTH, MOBA_HEADS * MOBA_HEAD_DIM, D_MODEL), (MOBA_HEADS * MOBA_HEAD_DIM) ** -0.5),
        'w_o': nrm(ks[19], (DEPTH, D_MODEL, D_MODEL), D_MODEL ** -0.5),
        'g_ffn': gain(ks[20], (DEPTH, D_MODEL)),
        'peer_wq': nrm(ks[21], (DEPTH, D_MODEL, PEER_HEADS * PEER_QDIM), D_MODEL ** -0.5),
        'peer_subkeys': nrm(ks[22], (DEPTH, PEER_HEADS, 2, PEER_NKEYS, PEER_QDIM // 2), (PEER_QDIM // 2) ** -0.5),
        'peer_u': nrm(ks[23], (DEPTH, PEER_N_EXPERTS, D_MODEL), D_MODEL ** -0.5),
        'peer_v': nrm(ks[24], (DEPTH, PEER_N_EXPERTS, D_MODEL), PEER_HEADS ** -0.5),
        'g_ple': gain(ks[25], (DEPTH, D_MODEL)),
        'w_ple_gate': nrm(ks[26], (DEPTH, D_MODEL, D_MODEL), D_MODEL ** -0.5),
        'w_ple_proj': nrm(ks[27], (DEPTH, PLE_DIM, D_MODEL), PLE_DIM ** -0.5),
        'g_final': gain(ks[28], (D_MODEL,)),
    }


def reference(x_prompt, x_sample, cache_mla_ckv, cache_mla_krope, cache_moba_k, cache_moba_v, page_table,
              p_prompt, p_sample, rel_bias, g_mix, w_in, g_q_lat, w_uq, g_kv_lat, w_uk, w_uv, w_a_out,
              w_b_out, w_o, g_ffn, peer_wq, peer_subkeys, peer_u, peer_v, g_ple, w_ple_gate, w_ple_proj, g_final):
    past_len = page_table.shape[1] * cache_mla_ckv.shape[2]
    pos_p = jnp.arange(x_prompt.shape[1])
    pos_s = past_len + jnp.arange(x_sample.shape[1])
    xp, xs = x_prompt, x_sample
    ckv_p, kr_p, k_p, v_p = [], [], [], []
    ckv_s, kr_s, k_s, v_s = [], [], [], []
    for l in range(DEPTH):
        qa, qr, c, kr, mq, mk, mv, za, zb = mixer_inputs(rms_norm(xp, g_mix[l]), pos_p, w_in[l], g_q_lat[l],
                                                         w_uq[l], g_kv_lat[l], w_uk[l])
        o_lat = mla_prompt(qa, qr, c, kr)
        o_moba = moba_prompt(mq, mk, mv, rel_bias)
        xp = mixer_merge(xp, o_lat, o_moba, za, zb, w_uv[l], w_a_out[l], w_b_out[l], w_o[l])
        xp = channel_and_ple(xp, p_prompt[l], g_ffn[l], peer_wq[l], peer_subkeys[l], peer_u[l], peer_v[l],
                             g_ple[l], w_ple_gate[l], w_ple_proj[l])
        ckv_p.append(c); kr_p.append(kr); k_p.append(mk); v_p.append(mv)
        qa, qr, c, kr, mq, mk, mv, za, zb = mixer_inputs(rms_norm(xs, g_mix[l]), pos_s, w_in[l], g_q_lat[l],
                                                         w_uq[l], g_kv_lat[l], w_uk[l])
        o_lat, o_moba = sample_attention(l, page_table, qa, qr, c, kr, mq, mk, mv, cache_mla_ckv,
                                         cache_mla_krope, cache_moba_k, cache_moba_v, rel_bias)
        xs = mixer_merge(xs, o_lat, o_moba, za, zb, w_uv[l], w_a_out[l], w_b_out[l], w_o[l])
        xs = channel_and_ple(xs, p_sample[l], g_ffn[l], peer_wq[l], peer_subkeys[l], peer_u[l], peer_v[l],
                             g_ple[l], w_ple_gate[l], w_ple_proj[l])
        ckv_s.append(c); kr_s.append(kr); k_s.append(mk); v_s.append(mv)
    return (rms_norm(xp, g_final), rms_norm(xs, g_final),
            jnp.stack(ckv_p), jnp.stack(kr_p), jnp.stack(k_p), jnp.stack(v_p),
            jnp.stack(ckv_s), jnp.stack(kr_s), jnp.stack(k_s), jnp.stack(v_s))
```

```python
import functools
import math

import numpy as np
import jax
import jax.numpy as jnp
from jax import lax
from jax.experimental import pallas as pl
from jax.experimental.pallas import tpu as pltpu

F32 = jnp.float32
BF16 = jnp.bfloat16

D_MODEL = 2048
MLA_HEADS = 8
MLA_LORA = 512
MLA_NOPE = 128
MLA_ROPE = 64
MLA_V = 128
ROPE_THETA = 10000.0
MOBA_HEADS = 8
MOBA_KV_HEADS = 2
MOBA_GROUP = MOBA_HEADS // MOBA_KV_HEADS
MOBA_DIM = 128
MOBA_BLOCK = 256
MOBA_TOPK = 3
REL_BUCKETS = 32
REL_MAX_DIST = 128
PEER_HEADS = 8
PEER_NKEYS = 128
PEER_TOPK = 16
PLE_DIM = 256
PAGE = 128
NORM_EPS = 1e-6
NEG_INF = -1e30
MASK_BIG = -(2.0 ** 100)

LANES = 128
VMEM_LIMIT = 56 * 1024 * 1024

OFF_A = 0
OFF_B = D_MODEL
OFF_Q = 2 * D_MODEL
OFF_C = OFF_Q + MLA_LORA
OFF_MQ = OFF_C + MLA_LORA
OFF_MK = OFF_MQ + MOBA_HEADS * MOBA_DIM
OFF_MV = OFF_MK + MOBA_KV_HEADS * MOBA_DIM
OFF_R = OFF_MV + MOBA_KV_HEADS * MOBA_DIM
Z_WIDTH = 6912
Z_TILE = 1152


def _cparams(sem):
    return pltpu.CompilerParams(dimension_semantics=sem, vmem_limit_bytes=VMEM_LIMIT)


def _resident(shape):
    nd = len(shape)
    return pl.BlockSpec(shape, lambda *_: (0,) * nd, pipeline_mode=pl.Buffered(1))


def _dot(a, b):
    return jnp.dot(a, b, preferred_element_type=F32)


def _dot_nt(a, b):
    return lax.dot_general(a, b, (((1,), (1,)), ((), ())), preferred_element_type=F32)


def _rms(x, g):
    return x * lax.rsqrt(jnp.mean(x * x, axis=-1, keepdims=True) + NORM_EPS) * g


def _sigmoid(x):
    return 1.0 / (1.0 + jnp.exp(-x))


def _norm_matmul_kernel(x_ref, g_ref, w_ref, o_ref, h_ref):
    @pl.when(pl.program_id(1) == 0)
    def _():
        h_ref[...] = _rms(x_ref[...], g_ref[...]).astype(BF16)

    o_ref[...] = _dot(h_ref[...], w_ref[...])


def _norm_matmul(x, g, w, tn):
    m, k = x.shape
    n = w.shape[1]
    tm = min(512, m)
    return pl.pallas_call(
        _norm_matmul_kernel,
        grid=(m // tm, n // tn),
        in_specs=[pl.BlockSpec((tm, k), lambda i, j: (i, 0)),
                  pl.BlockSpec((1, k), lambda i, j: (0, 0)),
                  pl.BlockSpec((k, tn), lambda i, j: (0, j))],
        out_specs=pl.BlockSpec((tm, tn), lambda i, j: (i, j)),
        out_shape=jax.ShapeDtypeStruct((m, n), F32),
        scratch_shapes=[pltpu.VMEM((tm, k), BF16)],
        compiler_params=_cparams(("parallel", "arbitrary")),
        name="norm_matmul",
    )(x, g, w)


def _mla_prep_kernel(zq_ref, zc_ref, zr_ref, cos_ref, sin_ref, gq_ref, wuq_ref, gkv_ref, wukt_ref,
                     sela_ref, selb_ref, ckv_ref, krope_ref, kc_ref, kr_ref, qa_ref, qr_ref):
    nope = MLA_HEADS * MLA_NOPE
    half = MLA_HEADS * MLA_ROPE // 2
    q_all = _dot(_rms(zq_ref[...], gq_ref[...]).astype(BF16), wuq_ref[...])
    cos = cos_ref[...]
    sin = sin_ref[...]
    x1 = q_all[:, nope:nope + half]
    x2 = q_all[:, nope + half:]
    r1 = (x1 * cos - x2 * sin).astype(BF16)
    r2 = (x2 * cos + x1 * sin).astype(BF16)
    q_rope = _dot(r1, sela_ref[...]) + _dot(r2, selb_ref[...])
    for h in range(MLA_HEADS):
        qr_ref[h] = q_rope[:, h * LANES:(h + 1) * LANES].astype(BF16)
        q_nope = q_all[:, h * MLA_NOPE:(h + 1) * MLA_NOPE].astype(BF16)
        qa_ref[h] = _dot(q_nope, wukt_ref[h]).astype(BF16)

    ckv = _rms(zc_ref[...], gkv_ref[...])
    ckv_ref[...] = ckv
    kc_ref[...] = ckv.astype(BF16)

    zr = zr_ref[...]
    k1 = zr[:, :MLA_ROPE // 2]
    k2 = zr[:, MLA_ROPE // 2:MLA_ROPE]
    ck = cos[:, :MLA_ROPE // 2]
    sk = sin[:, :MLA_ROPE // 2]
    kr = jnp.concatenate([k1 * ck - k2 * sk, k2 * ck + k1 * sk], axis=1)
    krope_ref[...] = kr
    kr_ref[...] = jnp.concatenate([kr, jnp.zeros_like(kr)], axis=1).astype(BF16)


def _mla_prep(z, cos, sin, gq, wuq, gkv, wukt, sela, selb):
    t = z.shape[0]
    tm = min(256, t)
    row = lambda c: (lambda i: (i, c))
    hrow = lambda i: (0, i, 0)
    return pl.pallas_call(
        _mla_prep_kernel,
        grid=(t // tm,),
        in_specs=[pl.BlockSpec((tm, MLA_LORA), row(OFF_Q // MLA_LORA)),
                  pl.BlockSpec((tm, MLA_LORA), row(OFF_C // MLA_LORA)),
                  pl.BlockSpec((tm, LANES), row(OFF_R // LANES)),
                  pl.BlockSpec((tm, 256), row(0)),
                  pl.BlockSpec((tm, 256), row(0)),
                  _resident(gq.shape), _resident(wuq.shape), _resident(gkv.shape),
                  _resident(wukt.shape), _resident(sela.shape), _resident(selb.shape)],
        out_specs=[pl.BlockSpec((tm, MLA_LORA), row(0)),
                   pl.BlockSpec((tm, MLA_ROPE), row(0)),
                   pl.BlockSpec((tm, MLA_LORA), row(0)),
                   pl.BlockSpec((tm, LANES), row(0)),
                   pl.BlockSpec((MLA_HEADS, tm, MLA_LORA), hrow),
                   pl.BlockSpec((MLA_HEADS, tm, LANES), hrow)],
        out_shape=[jax.ShapeDtypeStruct((t, MLA_LORA), F32),
                   jax.ShapeDtypeStruct((t, MLA_ROPE), F32),
                   jax.ShapeDtypeStruct((t, MLA_LORA), BF16),
                   jax.ShapeDtypeStruct((t, LANES), BF16),
                   jax.ShapeDtypeStruct((MLA_HEADS, t, MLA_LORA), BF16),
                   jax.ShapeDtypeStruct((MLA_HEADS, t, LANES), BF16)],
        compiler_params=_cparams(("parallel",)),
        name="mla_prep",
    )(z, z, z, cos, sin, gq, wuq, gkv, wukt, sela, selb)


def _softmax_step(s, m_ref, l_ref, acc_ref, pv):
    m_prev = m_ref[...]
    m_new = jnp.maximum(m_prev, jnp.max(s, axis=-1, keepdims=True))
    alpha = jnp.exp(m_prev - m_new)
    p = jnp.exp(s - m_new)
    l_ref[...] = alpha * l_ref[...] + jnp.sum(p, axis=-1, keepdims=True)
    acc_ref[...] = alpha * acc_ref[...] + pv(p.astype(BF16))
    m_ref[...] = m_new


def _softmax_init(m_ref, l_ref, acc_ref):
    m_ref[...] = jnp.full(m_ref.shape, -jnp.inf, F32)
    l_ref[...] = jnp.zeros(l_ref.shape, F32)
    acc_ref[...] = jnp.zeros(acc_ref.shape, F32)


def _mla_attn_kernel(qa_ref, qr_ref, kc_ref, kr_ref, wuv_ref, o_ref, m_ref, l_ref, acc_ref, *, tq, tk):
    i = pl.program_id(1)
    j = pl.program_id(2)
    last = (i * tq + tq - 1) // tk
    rows = MLA_HEADS * tq
    scale = (MLA_NOPE + MLA_ROPE) ** -0.5

    @pl.when(j == 0)
    def _():
        _softmax_init(m_ref, l_ref, acc_ref)

    def scores():
        qa = qa_ref[...].reshape(rows, MLA_LORA)
        qr = qr_ref[...].reshape(rows, LANES)
        return (_dot_nt(qa, kc_ref[...]) + _dot_nt(qr, kr_ref[...])) * scale

    def pv(p):
        return _dot(p, kc_ref[...])

    @pl.when(j < last)
    def _():
        _softmax_step(scores(), m_ref, l_ref, acc_ref, pv)

    @pl.when(j == last)
    def _():
        s = scores()
        qpos = i * tq + (lax.broadcasted_iota(jnp.int32, s.shape, 0) & (tq - 1))
        kpos = j * tk + lax.broadcasted_iota(jnp.int32, s.shape, 1)
        s = jnp.where(kpos <= qpos, s, NEG_INF)
        _softmax_step(s, m_ref, l_ref, acc_ref, pv)
        o_lat = (acc_ref[...] / l_ref[...]).astype(BF16)
        for h in range(MLA_HEADS):
            o_ref[:, h * MLA_V:(h + 1) * MLA_V] = _dot(o_lat[h * tq:(h + 1) * tq], wuv_ref[h]).astype(BF16)


def _mla_prompt_attn(qa, qr, kc, kr, wuv, batch, seq):
    tq = min(256, seq)
    tk = min(512, seq)
    nq, nk = seq // tq, seq // tk
    t = batch * seq

    def qmap(b, i, j):
        return (0, b * nq + i, 0)

    def kmap(b, i, j):
        return (b * nk + jnp.minimum(j, (i * tq + tq - 1) // tk), 0)

    rows = MLA_HEADS * tq
    return pl.pallas_call(
        functools.partial(_mla_attn_kernel, tq=tq, tk=tk),
        grid=(batch, nq, nk),
        in_specs=[pl.BlockSpec((MLA_HEADS, tq, MLA_LORA), qmap),
                  pl.BlockSpec((MLA_HEADS, tq, LANES), qmap),
                  pl.BlockSpec((tk, MLA_LORA), kmap),
                  pl.BlockSpec((tk, LANES), kmap),
                  _resident(wuv.shape)],
        out_specs=pl.BlockSpec((tq, MLA_HEADS * MLA_V), lambda b, i, j: (b * nq + i, 0)),
        out_shape=jax.ShapeDtypeStruct((t, MLA_HEADS * MLA_V), BF16),
        scratch_shapes=[pltpu.VMEM((rows, 1), F32), pltpu.VMEM((rows, 1), F32),
                        pltpu.VMEM((rows, MLA_LORA), F32)],
        compiler_params=_cparams(("parallel", "parallel", "arbitrary")),
        name="mla_prompt_attn",
    )(qa, qr, kc, kr, wuv)


def _moba_kprep_kernel(zk_ref, zv_ref, mean_ref, km_ref, vm_ref, *, nblk):
    n = pl.program_id(0) % nblk
    k = zk_ref[...]
    v = zv_ref[...]
    mean_ref[0] = jnp.mean(k, axis=0, keepdims=True)
    onehot = jnp.where(lax.broadcasted_iota(jnp.int32, (MOBA_BLOCK, LANES), 1) == n, 1.0, 0.0).astype(BF16)
    for g in range(MOBA_KV_HEADS):
        km_ref[g] = jnp.concatenate([k[:, g * MOBA_DIM:(g + 1) * MOBA_DIM].astype(BF16), onehot], axis=1)
        vm_ref[g] = v[:, g * MOBA_DIM:(g + 1) * MOBA_DIM].astype(BF16)


def _moba_kprep(z, seq):
    t = z.shape[0]
    nb = t // MOBA_BLOCK
    kvw = MOBA_KV_HEADS * MOBA_DIM
    return pl.pallas_call(
        functools.partial(_moba_kprep_kernel, nblk=seq // MOBA_BLOCK),
        grid=(nb,),
        in_specs=[pl.BlockSpec((MOBA_BLOCK, kvw), lambda i: (i, OFF_MK // kvw)),
                  pl.BlockSpec((MOBA_BLOCK, kvw), lambda i: (i, OFF_MV // kvw))],
        out_specs=[pl.BlockSpec((1, 1, kvw), lambda i: (i, 0, 0)),
                   pl.BlockSpec((MOBA_KV_HEADS, MOBA_BLOCK, 2 * MOBA_DIM), lambda i: (0, i, 0)),
                   pl.BlockSpec((MOBA_KV_HEADS, MOBA_BLOCK, MOBA_DIM), lambda i: (0, i, 0))],
        out_shape=[jax.ShapeDtypeStruct((nb, 1, kvw), F32),
                   jax.ShapeDtypeStruct((MOBA_KV_HEADS, t, 2 * MOBA_DIM), BF16),
                   jax.ShapeDtypeStruct((MOBA_KV_HEADS, t, MOBA_DIM), BF16)],
        compiler_params=_cparams(("parallel",)),
        name="moba_kprep",
    )(z, z)


def _top_mask(s, k):
    lane = lax.broadcasted_iota(jnp.int32, s.shape, 1)
    sel = jnp.zeros(s.shape, jnp.bool_)
    for _ in range(k):
        m = jnp.max(s, axis=-1, keepdims=True)
        idx = jnp.min(jnp.where(s == m, lane, s.shape[1]), axis=-1, keepdims=True)
        hit = lane == idx
        sel = jnp.logical_or(sel, hit)
        s = jnp.where(hit, -jnp.inf, s)
    return sel


def _moba_select_kernel(zq_ref, km_ref, qm_ref, *, nblk):
    own = pl.program_id(0) % nblk
    zq = zq_ref[...]
    km = km_ref[...].astype(BF16)
    for h in range(MOBA_HEADS):
        g = h // MOBA_GROUP
        q = zq[:, h * MOBA_DIM:(h + 1) * MOBA_DIM].astype(BF16)
        s = _dot_nt(q, km[:, g * MOBA_DIM:(g + 1) * MOBA_DIM])
        lane = lax.broadcasted_iota(jnp.int32, s.shape, 1)
        past = lane < own
        sel = _top_mask(jnp.where(past, s, NEG_INF), MOBA_TOPK)
        mask = jnp.where(jnp.logical_and(past, jnp.logical_not(sel)), MASK_BIG, 0.0)
        qm_ref[h] = jnp.concatenate([q, mask.astype(BF16)], axis=1)


def _moba_select(z, kmean, seq):
    t = z.shape[0]
    nblk = seq // MOBA_BLOCK
    qw = MOBA_HEADS * MOBA_DIM
    return pl.pallas_call(
        functools.partial(_moba_select_kernel, nblk=nblk),
        grid=(t // MOBA_BLOCK,),
        in_specs=[pl.BlockSpec((MOBA_BLOCK, qw), lambda i: (i, OFF_MQ // qw)),
                  pl.BlockSpec((None, LANES, MOBA_KV_HEADS * MOBA_DIM), lambda i: (i // nblk, 0, 0))],
        out_specs=pl.BlockSpec((MOBA_HEADS, MOBA_BLOCK, 2 * MOBA_DIM), lambda i: (0, i, 0)),
        out_shape=jax.ShapeDtypeStruct((MOBA_HEADS, t, 2 * MOBA_DIM), BF16),
        compiler_params=_cparams(("parallel",)),
        name="moba_select",
    )(z, kmean)


def _bias_table_kernel(rb_ref, bucket_ref, o_ref):
    bucket = bucket_ref[...]
    for h in range(MOBA_HEADS):
        acc = jnp.zeros(bucket.shape, F32)
        for b in range(REL_BUCKETS):
            acc = jnp.where(bucket == b, rb_ref[b, h], acc)
        o_ref[h] = acc


def _bias_table(rel_bias, bucket):
    return pl.pallas_call(
        _bias_table_kernel,
        in_specs=[pl.BlockSpec(memory_space=pltpu.SMEM), pl.BlockSpec(memory_space=pltpu.VMEM)],
        out_specs=pl.BlockSpec(memory_space=pltpu.VMEM),
        out_shape=jax.ShapeDtypeStruct((MOBA_HEADS,) + bucket.shape, F32),
        name="bias_table",
    )(rel_bias, bucket)


def _moba_attn_kernel(rb_ref, qm_ref, km_ref, vm_ref, tb_ref, o_ref, m_ref, l_ref, acc_ref):
    g = pl.program_id(1)
    i = pl.program_id(2)
    j = pl.program_id(3)
    rows = MOBA_GROUP * MOBA_BLOCK
    scale = MOBA_DIM ** -0.5

    @pl.when(j == 0)
    def _():
        _softmax_init(m_ref, l_ref, acc_ref)

    def scores():
        q = qm_ref[...].reshape(rows, 2 * MOBA_DIM)
        return (_dot_nt(q, km_ref[0]) * scale).reshape(MOBA_GROUP, MOBA_BLOCK, MOBA_BLOCK)

    def step(s):
        _softmax_step(s.reshape(rows, MOBA_BLOCK), m_ref, l_ref, acc_ref, lambda p: _dot(p, vm_ref[0]))

    @pl.when(j == 0)
    def _():
        s = scores() + tb_ref[:, :, MOBA_BLOCK:]
        r = lax.broadcasted_iota(jnp.int32, s.shape, 1)
        c = lax.broadcasted_iota(jnp.int32, s.shape, 2)
        step(jnp.where(c <= r, s, NEG_INF))

    @pl.when(jnp.logical_and(j >= 1, j == i))
    def _():
        step(scores() + tb_ref[:, :, :MOBA_BLOCK])

    @pl.when(jnp.logical_and(j >= 1, j < i))
    def _():
        s = scores()
        far = jnp.stack([jnp.full((MOBA_BLOCK, MOBA_BLOCK), rb_ref[REL_BUCKETS - 1, g * MOBA_GROUP + hh], F32)
                         for hh in range(MOBA_GROUP)])
        step(s + far)

    @pl.when(j == pl.num_programs(3) - 1)
    def _():
        o = acc_ref[...] / l_ref[...]
        for hh in range(MOBA_GROUP):
            o_ref[:, hh * MOBA_DIM:(hh + 1) * MOBA_DIM] = o[hh * MOBA_BLOCK:(hh + 1) * MOBA_BLOCK].astype(BF16)


def _moba_prompt_attn(rel_bias, qm, km, vm, tb, batch, seq):
    nblk = seq // MOBA_BLOCK
    t = batch * seq
    rows = MOBA_GROUP * MOBA_BLOCK

    def kmap(b, g, i, j):
        kb = jnp.where(j == 0, i, jnp.minimum(j - 1, jnp.maximum(i - 1, 0)))
        return (g, b * nblk + kb, 0)

    return pl.pallas_call(
        _moba_attn_kernel,
        grid=(batch, MOBA_KV_HEADS, nblk, nblk),
        in_specs=[pl.BlockSpec(memory_space=pltpu.SMEM),
                  pl.BlockSpec((MOBA_GROUP, MOBA_BLOCK, 2 * MOBA_DIM), lambda b, g, i, j: (g, b * nblk + i, 0)),
                  pl.BlockSpec((1, MOBA_BLOCK, 2 * MOBA_DIM), kmap),
                  pl.BlockSpec((1, MOBA_BLOCK, MOBA_DIM), kmap),
                  pl.BlockSpec((MOBA_GROUP, MOBA_BLOCK, 2 * MOBA_BLOCK), lambda b, g, i, j: (g, 0, 0))],
        out_specs=pl.BlockSpec((MOBA_BLOCK, MOBA_GROUP * MOBA_DIM), lambda b, g, i, j: (b * nblk + i, g)),
        out_shape=jax.ShapeDtypeStruct((t, MOBA_HEADS * MOBA_DIM), BF16),
        scratch_shapes=[pltpu.VMEM((rows, 1), F32), pltpu.VMEM((rows, 1), F32),
                        pltpu.VMEM((rows, MOBA_DIM), F32)],
        compiler_params=_cparams(("parallel", "parallel", "parallel", "arbitrary")),
        name="moba_prompt_attn",
    )(rel_bias, qm, km, vm, tb)


def _merge_kernel(x_ref, oa_ref, ob_ref, za_ref, zb_ref, wa_ref, wb_ref, wo_ref, g_ref, x1_ref, h_ref):
    a = _dot(oa_ref[...], wa_ref[...])
    b = _dot(ob_ref[...], wb_ref[...])
    y = _sigmoid(za_ref[...]) * a + _sigmoid(zb_ref[...]) * b
    x1 = x_ref[...] + _dot(y.astype(BF16), wo_ref[...])
    x1_ref[...] = x1
    h_ref[...] = _rms(x1, g_ref[...]).astype(BF16)


def _merge(x, oa, ob, z, wa, wb, wo, g):
    t, d = x.shape
    tm = min(256, t)
    row = lambda c: (lambda i: (i, c))
    return pl.pallas_call(
        _merge_kernel,
        grid=(t // tm,),
        in_specs=[pl.BlockSpec((tm, d), row(0)),
                  pl.BlockSpec((tm, oa.shape[1]), row(0)),
                  pl.BlockSpec((tm, ob.shape[1]), row(0)),
                  pl.BlockSpec((tm, d), row(OFF_A // d)),
                  pl.BlockSpec((tm, d), row(OFF_B // d)),
                  _resident(wa.shape), _resident(wb.shape), _resident(wo.shape), _resident(g.shape)],
        out_specs=[pl.BlockSpec((tm, d), row(0)), pl.BlockSpec((tm, d), row(0))],
        out_shape=[jax.ShapeDtypeStruct((t, d), F32), jax.ShapeDtypeStruct((t, d), BF16)],
        compiler_params=_cparams(("parallel",)),
        name="mixer_merge",
    )(x, oa, ob, z, z, wa, wb, wo, g)


def _top_rows(s, k):
    row = lax.broadcasted_iota(jnp.int32, s.shape, 0)
    rest = s
    vals = []
    for _ in range(k):
        m = jnp.max(rest, axis=0, keepdims=True)
        idx = jnp.min(jnp.where(rest == m, row, s.shape[0]), axis=0, keepdims=True)
        rest = jnp.where(row == idx, -jnp.inf, rest)
        vals.append(m)
    return jnp.concatenate(vals, axis=0), jnp.where(rest == -jnp.inf, s, -jnp.inf)


def _peer_route_kernel(h_ref, wqt_ref, sk_ref, u0_ref, s1_ref, tau_ref, invz_ref):
    qt = _dot_nt(wqt_ref[...], h_ref[...])
    nk = PEER_NKEYS
    for p in range(PEER_HEADS):
        halves = []
        for half in range(2):
            g = 2 * p + half
            s = _dot(sk_ref[g], qt[g * nk:(g + 1) * nk].astype(BF16))
            halves.append(_top_rows(s, PEER_TOPK))
        (v0, s0), (v1, s1) = halves
        top = v0[0:1] + v1[0:1]
        u0 = s0 - top
        v0s = v0 - top
        cand = jnp.concatenate([v0s[a:a + 1] + v1 for a in range(PEER_TOPK)], axis=0)
        tau = _top_rows(cand, PEER_TOPK)[0][PEER_TOPK - 1:PEER_TOPK]
        z = jnp.sum(jnp.where(cand >= tau, jnp.exp(cand), 0.0), axis=0, keepdims=True)
        u0_ref[p] = u0
        s1_ref[p] = s1
        tau_ref[p:p + 1, :] = tau
        invz_ref[p:p + 1, :] = 1.0 / z


def _peer_route(h, wqt, sk):
    t, d = h.shape
    tt = min(256, t)
    nk = PEER_NKEYS
    tok = lambda i: (0, 0, i)
    return pl.pallas_call(
        _peer_route_kernel,
        grid=(t // tt,),
        in_specs=[pl.BlockSpec((tt, d), lambda i: (i, 0)), _resident(wqt.shape), _resident(sk.shape)],
        out_specs=[pl.BlockSpec((PEER_HEADS, nk, tt), tok), pl.BlockSpec((PEER_HEADS, nk, tt), tok),
                   pl.BlockSpec((PEER_HEADS, tt), lambda i: (0, i)),
                   pl.BlockSpec((PEER_HEADS, tt), lambda i: (0, i))],
        out_shape=[jax.ShapeDtypeStruct((PEER_HEADS, nk, t), F32), jax.ShapeDtypeStruct((PEER_HEADS, nk, t), F32),
                   jax.ShapeDtypeStruct((PEER_HEADS, t), F32), jax.ShapeDtypeStruct((PEER_HEADS, t), F32)],
        compiler_params=_cparams(("parallel",)),
        name="peer_route",
    )(h, wqt, sk)


PEER_EXPERT_TILE = 1024
PEER_ROWS_PER_TILE = PEER_EXPERT_TILE // PEER_NKEYS


def _peer_dense_kernel(h_ref, u_ref, vt_ref, u0_ref, s1_ref, tau_ref, invz_ref, o_ref, acc_ref, w_ref):
    e = pl.program_id(1)

    @pl.when(e == 0)
    def _():
        acc_ref[...] = jnp.zeros(acc_ref.shape, F32)

    at = _dot_nt(u_ref[...], h_ref[...])
    act = 0.5 * at * (1.0 + lax.erf(at * (2.0 ** -0.5)))
    nk = PEER_NKEYS
    for ii in range(PEER_ROWS_PER_TILE):
        gate = jnp.zeros((nk, at.shape[1]), F32)
        for p in range(PEER_HEADS):
            val = u0_ref[p, ii:ii + 1, :] + s1_ref[p]
            gate = gate + jnp.where(val >= tau_ref[p:p + 1, :], jnp.exp(val) * invz_ref[p:p + 1, :], 0.0)
        w_ref[ii * nk:(ii + 1) * nk, :] = (gate * act[ii * nk:(ii + 1) * nk]).astype(BF16)
    acc_ref[...] += _dot(vt_ref[...], w_ref[...])

    @pl.when(e == pl.num_programs(1) - 1)
    def _():
        o_ref[...] = acc_ref[...].T


def _peer_dense(h, u, vt, u0, s1, tau, invz):
    t, d = h.shape
    tt = min(512, t)
    te = PEER_EXPERT_TILE
    n_exp = u.shape[0]
    nk = PEER_NKEYS
    return pl.pallas_call(
        _peer_dense_kernel,
        grid=(t // tt, n_exp // te),
        in_specs=[pl.BlockSpec((tt, d), lambda i, e: (i, 0)),
                  pl.BlockSpec((te, d), lambda i, e: (e, 0)),
                  pl.BlockSpec((d, te), lambda i, e: (0, e)),
                  pl.BlockSpec((PEER_HEADS, PEER_ROWS_PER_TILE, tt), lambda i, e: (0, e, i)),
                  pl.BlockSpec((PEER_HEADS, nk, tt), lambda i, e: (0, 0, i)),
                  pl.BlockSpec((PEER_HEADS, tt), lambda i, e: (0, i)),
                  pl.BlockSpec((PEER_HEADS, tt), lambda i, e: (0, i))],
        out_specs=pl.BlockSpec((tt, d), lambda i, e: (i, 0)),
        out_shape=jax.ShapeDtypeStruct((t, d), F32),
        scratch_shapes=[pltpu.VMEM((d, tt), F32), pltpu.VMEM((te, tt), BF16)],
        compiler_params=_cparams(("parallel", "arbitrary")),
        name="peer_dense",
    )(h, u, vt, u0, s1, tau, invz)


def _ple_final_kernel(x_ref, f_ref, p_ref, gp_ref, wg_ref, wp_ref, gf_ref, o_ref):
    x = x_ref[...] + f_ref[...]
    gate = _sigmoid(_dot(_rms(x, gp_ref[...]).astype(BF16), wg_ref[...]))
    x = x + gate * _dot(p_ref[...].astype(BF16), wp_ref[...])
    o_ref[...] = _rms(x, gf_ref[...])


def _ple_final(x, ffn, p, gp, wg, wp, gf):
    t, d = x.shape
    tm = min(256, t)
    return pl.pallas_call(
        _ple_final_kernel,
        grid=(t // tm,),
        in_specs=[pl.BlockSpec((tm, d), lambda i: (i, 0)),
                  pl.BlockSpec((tm, d), lambda i: (i, 0)),
                  pl.BlockSpec((tm, p.shape[1]), lambda i: (i, 0)),
                  _resident(gp.shape), _resident(wg.shape), _resident(wp.shape), _resident(gf.shape)],
        out_specs=pl.BlockSpec((tm, d), lambda i: (i, 0)),
        out_shape=jax.ShapeDtypeStruct((t, d), F32),
        compiler_params=_cparams(("parallel",)),
        name="ple_final",
    )(x, ffn, p, gp, wg, wp, gf)


NEW_PAD = LANES
MLA_PAGES_PER_STEP = 8
MOBA_PAGES_PER_STEP = 16


def _page_spec(width, k, per_step):
    return pl.BlockSpec((None, None, PAGE, width), lambda b, j, pt: (0, pt[b, j * per_step + k], 0, 0))


def _mla_sample_kernel(pt_ref, qa_ref, qr_ref, kcn_ref, krn_ref, *rest, ds):
    pg = MLA_PAGES_PER_STEP
    c_refs, r_refs = rest[:pg], rest[pg:2 * pg]
    o_ref, m_ref, l_ref, acc_ref = rest[2 * pg:]
    j = pl.program_id(1)
    scale = (MLA_NOPE + MLA_ROPE) ** -0.5

    @pl.when(j == 0)
    def _():
        _softmax_init(m_ref, l_ref, acc_ref)

    qa = qa_ref[...]
    qr = qr_ref[...]
    cs = [c[...].astype(BF16) for c in c_refs]
    s = jnp.concatenate([_dot_nt(qa, cs[k]) + _dot_nt(qr, r_refs[k][...].astype(BF16)) for k in range(pg)],
                        axis=1) * scale

    def pv(p):
        out = _dot(p[:, :PAGE], cs[0])
        for k in range(1, pg):
            out = out + _dot(p[:, k * PAGE:(k + 1) * PAGE], cs[k])
        return out

    _softmax_step(s, m_ref, l_ref, acc_ref, pv)

    @pl.when(j == pl.num_programs(1) - 1)
    def _():
        kcn = kcn_ref[...]
        sn = (_dot_nt(qa, kcn) + _dot_nt(qr, krn_ref[...])) * scale
        tok = lax.broadcasted_iota(jnp.int32, sn.shape, 0) & (ds - 1)
        col = lax.broadcasted_iota(jnp.int32, sn.shape, 1)
        sn = jnp.where(col <= tok, sn, NEG_INF)
        _softmax_step(sn, m_ref, l_ref, acc_ref, lambda p: _dot(p, kcn))
        o_ref[...] = acc_ref[...] / l_ref[...]


def _mla_sample_attn(page_table, qa, qr, kcn, krn, cache_ckv, cache_krope, ds):
    nseq, rows, _ = qa.shape
    pg = MLA_PAGES_PER_STEP
    nsteps = page_table.shape[1] // pg
    seq3 = lambda b, j, pt: (b, 0, 0)
    grid_spec = pltpu.PrefetchScalarGridSpec(
        num_scalar_prefetch=1,
        grid=(nseq, nsteps),
        in_specs=[pl.BlockSpec((None, rows, MLA_LORA), seq3),
                  pl.BlockSpec((None, rows, MLA_ROPE), seq3),
                  pl.BlockSpec((None, NEW_PAD, MLA_LORA), seq3),
                  pl.BlockSpec((None, NEW_PAD, MLA_ROPE), seq3)]
                 + [_page_spec(MLA_LORA, k, pg) for k in range(pg)]
                 + [_page_spec(MLA_ROPE, k, pg) for k in range(pg)],
        out_specs=pl.BlockSpec((None, rows, MLA_LORA), seq3),
        scratch_shapes=[pltpu.VMEM((rows, 1), F32), pltpu.VMEM((rows, 1), F32), pltpu.VMEM((rows, MLA_LORA), F32)],
    )
    return pl.pallas_call(
        functools.partial(_mla_sample_kernel, ds=ds),
        grid_spec=grid_spec,
        out_shape=jax.ShapeDtypeStruct((nseq, rows, MLA_LORA), F32),
        compiler_params=_cparams(("parallel", "arbitrary")),
        name="mla_sample_attn",
    )(page_table, qa, qr, kcn, krn, *([cache_ckv] * pg), *([cache_krope] * pg))


def _head_proj_kernel(o_ref, w_ref, out_ref):
    for h in range(MLA_HEADS):
        out_ref[:, h * MLA_V:(h + 1) * MLA_V] = _dot(o_ref[h], w_ref[h]).astype(BF16)


def _head_proj(o_lat, wuv):
    t = o_lat.shape[1]
    return pl.pallas_call(
        _head_proj_kernel,
        out_shape=jax.ShapeDtypeStruct((t, MLA_HEADS * MLA_V), BF16),
        name="mla_head_proj",
    )(o_lat, wuv)


def _moba_sample_select_kernel(pt_ref, q_ref, *rest, nblk):
    pg = MOBA_PAGES_PER_STEP
    k_refs = rest[:pg]
    mask_ref, mean_ref = rest[pg:]
    j = pl.program_id(1)
    bps = pg // 2

    @pl.when(j == 0)
    def _():
        mean_ref[...] = jnp.zeros(mean_ref.shape, F32)

    sums = [jnp.sum(k_refs[2 * n][...], axis=0, keepdims=True) + jnp.sum(k_refs[2 * n + 1][...], axis=0, keepdims=True)
            for n in range(bps)]
    mean_ref[pl.ds(pl.multiple_of(j * bps, bps), bps), :] = jnp.concatenate(sums, axis=0) * (1.0 / MOBA_BLOCK)

    @pl.when(j == pl.num_programs(1) - 1)
    def _():
        km = mean_ref[...].astype(BF16)
        rows = q_ref.shape[0] // MOBA_KV_HEADS
        for g in range(MOBA_KV_HEADS):
            s = _dot_nt(q_ref[g * rows:(g + 1) * rows, :], km[:, g * MOBA_DIM:(g + 1) * MOBA_DIM])
            past = lax.broadcasted_iota(jnp.int32, s.shape, 1) < nblk
            sel = _top_mask(jnp.where(past, s, NEG_INF), min(MOBA_TOPK, nblk))
            mask = jnp.where(jnp.logical_and(past, jnp.logical_not(sel)), MASK_BIG, 0.0)
            mask_ref[g * rows:(g + 1) * rows, :] = mask.astype(BF16)


def _moba_sample_select(page_table, q, cache_k):
    nseq, rows, _ = q.shape
    pg = MOBA_PAGES_PER_STEP
    nsteps = page_table.shape[1] // pg
    nblk = page_table.shape[1] * PAGE // MOBA_BLOCK
    kvw = MOBA_KV_HEADS * MOBA_DIM
    seq3 = lambda b, j, pt: (b, 0, 0)
    grid_spec = pltpu.PrefetchScalarGridSpec(
        num_scalar_prefetch=1,
        grid=(nseq, nsteps),
        in_specs=[pl.BlockSpec((None, rows, MOBA_DIM), seq3)] + [_page_spec(kvw, k, pg) for k in range(pg)],
        out_specs=pl.BlockSpec((None, rows, LANES), seq3),
        scratch_shapes=[pltpu.VMEM((LANES, kvw), F32)],
    )
    return pl.pallas_call(
        functools.partial(_moba_sample_select_kernel, nblk=nblk),
        grid_spec=grid_spec,
        out_shape=jax.ShapeDtypeStruct((nseq, rows, LANES), BF16),
        compiler_params=_cparams(("parallel", "arbitrary")),
        name="moba_sample_select",
    )(page_table, q, *([cache_k] * pg))


def _moba_sample_attn_kernel(pt_ref, q_ref, mask_ref, exp_ref, bias_ref, kn_ref, vn_ref, bown_ref, *rest, ds):
    pg = MOBA_PAGES_PER_STEP
    k_refs, v_refs = rest[:pg], rest[pg:2 * pg]
    o_ref, m_ref, l_ref, acc_ref = rest[2 * pg:]
    j = pl.program_id(1)
    scale = MOBA_DIM ** -0.5
    rows = q_ref.shape[0] // MOBA_KV_HEADS

    @pl.when(j == 0)
    def _():
        _softmax_init(m_ref, l_ref, acc_ref)

    def group(x, g):
        return x[:, g * MOBA_DIM:(g + 1) * MOBA_DIM]

    def qg(g):
        return q_ref[g * rows:(g + 1) * rows, :]

    s = jnp.concatenate(
        [jnp.concatenate([_dot_nt(qg(g), group(k_refs[k][...], g).astype(BF16)) for k in range(pg)], axis=1)
         for g in range(MOBA_KV_HEADS)], axis=0)
    s = s * scale + bias_ref[...] + _dot(mask_ref[...], exp_ref[...])

    def pv(p):
        outs = []
        for g in range(MOBA_KV_HEADS):
            pgp = p[g * rows:(g + 1) * rows]
            out = _dot(pgp[:, :PAGE], group(v_refs[0][...], g).astype(BF16))
            for k in range(1, pg):
                out = out + _dot(pgp[:, k * PAGE:(k + 1) * PAGE], group(v_refs[k][...], g).astype(BF16))
            outs.append(out)
        return jnp.concatenate(outs, axis=0)

    _softmax_step(s, m_ref, l_ref, acc_ref, pv)

    @pl.when(j == pl.num_programs(1) - 1)
    def _():
        kn = kn_ref[...]
        vn = vn_ref[...]
        sn = jnp.concatenate([_dot_nt(qg(g), group(kn, g)) for g in range(MOBA_KV_HEADS)], axis=0)
        sn = sn * scale + bown_ref[...]
        tok = lax.broadcasted_iota(jnp.int32, sn.shape, 0) & (ds - 1)
        col = lax.broadcasted_iota(jnp.int32, sn.shape, 1)
        sn = jnp.where(col <= tok, sn, NEG_INF)

        def pv_new(p):
            return jnp.concatenate([_dot(p[g * rows:(g + 1) * rows], group(vn, g)) for g in range(MOBA_KV_HEADS)],
                                   axis=0)

        _softmax_step(sn, m_ref, l_ref, acc_ref, pv_new)
        o_ref[...] = acc_ref[...] / l_ref[...]


def _moba_sample_attn(page_table, q, mask, expand, bias, kn, vn, bown, cache_k, cache_v, ds):
    nseq, rows, _ = q.shape
    pg = MOBA_PAGES_PER_STEP
    nsteps = page_table.shape[1] // pg
    kvw = MOBA_KV_HEADS * MOBA_DIM
    keys = pg * PAGE
    seq3 = lambda b, j, pt: (b, 0, 0)
    last = nsteps - 1
    grid_spec = pltpu.PrefetchScalarGridSpec(
        num_scalar_prefetch=1,
        grid=(nseq, nsteps),
        in_specs=[pl.BlockSpec((None, rows, MOBA_DIM), seq3),
                  pl.BlockSpec((None, rows, LANES), seq3),
                  pl.BlockSpec((None, LANES, keys), lambda b, j, pt: (j, 0, 0)),
                  pl.BlockSpec((None, rows, keys), lambda b, j, pt: (jnp.where(j == last, 1, 0), 0, 0)),
                  pl.BlockSpec((None, NEW_PAD, kvw), seq3),
                  pl.BlockSpec((None, NEW_PAD, kvw), seq3),
                  pl.BlockSpec((rows, NEW_PAD), lambda b, j, pt: (0, 0))]
                 + [_page_spec(kvw, k, pg) for k in range(pg)]
                 + [_page_spec(kvw, k, pg) for k in range(pg)],
        out_specs=pl.BlockSpec((None, rows, MOBA_DIM), seq3),
        scratch_shapes=[pltpu.VMEM((rows, 1), F32), pltpu.VMEM((rows, 1), F32), pltpu.VMEM((rows, MOBA_DIM), F32)],
    )
    return pl.pallas_call(
        functools.partial(_moba_sample_attn_kernel, ds=ds),
        grid_spec=grid_spec,
        out_shape=jax.ShapeDtypeStruct((nseq, rows, MOBA_DIM), F32),
        compiler_params=_cparams(("parallel", "arbitrary")),
        name="moba_sample_attn",
    )(page_table, q, mask, expand, bias, kn, vn, bown, *([cache_k] * pg), *([cache_v] * pg))


def _rope_tables(pos):
    inv = jnp.exp(-math.log(ROPE_THETA) * jnp.arange(0, MLA_ROPE, 2, dtype=F32) / MLA_ROPE)
    ang = pos.astype(F32)[:, None] * inv[None, :]
    return jnp.tile(jnp.cos(ang), (1, MLA_HEADS)), jnp.tile(jnp.sin(ang), (1, MLA_HEADS))


def _t5_bucket(dist):
    dist = jnp.maximum(dist, 0)
    max_exact = REL_BUCKETS // 2
    scaled = (jnp.log(jnp.maximum(dist, 1).astype(F32) / max_exact)
              / math.log(REL_MAX_DIST / max_exact) * (REL_BUCKETS - max_exact))
    large = jnp.minimum(max_exact + scaled.astype(jnp.int32), REL_BUCKETS - 1)
    return jnp.where(dist < max_exact, dist, large)


def _rope_select_matrices():
    half = MLA_ROPE // 2
    sela = np.zeros((MLA_HEADS * half, MLA_HEADS * LANES), np.float32)
    selb = np.zeros_like(sela)
    for h in range(MLA_HEADS):
        for r in range(half):
            sela[h * half + r, h * LANES + r] = 1.0
            selb[h * half + r, h * LANES + half + r] = 1.0
    return jnp.asarray(sela, BF16), jnp.asarray(selb, BF16)


def _prepare_weights(w_in, w_uq, w_uk, w_uv, w_a_out, w_b_out, w_o, peer_wq, peer_subkeys, peer_u, peer_v,
                     w_ple_gate, w_ple_proj):
    d = D_MODEL
    lo = MLA_LORA
    o_q, o_c, o_r = 0, lo, 2 * lo
    o_mq = o_r + MLA_ROPE
    o_mk = o_mq + MOBA_HEADS * MOBA_DIM
    o_mv = o_mk + MOBA_KV_HEADS * MOBA_DIM
    o_a = o_mv + MOBA_KV_HEADS * MOBA_DIM
    o_b = o_a + d
    cols = [w_in[:, o_a:o_b], w_in[:, o_b:o_b + d], w_in[:, o_q:o_c], w_in[:, o_c:o_r], w_in[:, o_mq:o_mk],
            w_in[:, o_mk:o_mv], w_in[:, o_mv:o_a], w_in[:, o_r:o_mq]]
    used = sum(c.shape[1] for c in cols)
    w_in_p = jnp.concatenate(cols + [jnp.zeros((d, Z_WIDTH - used), w_in.dtype)], axis=1).astype(BF16)

    hd = MLA_NOPE + MLA_ROPE
    half = MLA_ROPE // 2
    uq = w_uq.reshape(lo, MLA_HEADS, hd)
    w_uq_p = jnp.concatenate([uq[:, :, :MLA_NOPE].reshape(lo, -1), uq[:, :, MLA_NOPE:MLA_NOPE + half].reshape(lo, -1),
                              uq[:, :, MLA_NOPE + half:].reshape(lo, -1)], axis=1).astype(BF16)
    sela, selb = _rope_select_matrices()
    return dict(
        w_in=w_in_p, w_uq=w_uq_p, sela=sela, selb=selb,
        w_ukt=jnp.transpose(w_uk, (1, 2, 0)).astype(BF16),
        w_uv=jnp.transpose(w_uv, (1, 0, 2)).astype(BF16),
        w_a=w_a_out.astype(BF16), w_b=w_b_out.astype(BF16), w_o=w_o.astype(BF16),
        wqt=peer_wq.T.astype(BF16),
        sk=peer_subkeys.reshape(2 * PEER_HEADS, PEER_NKEYS, -1).astype(BF16),
        u=peer_u.astype(BF16), vt=peer_v.T.astype(BF16),
        w_gate=w_ple_gate.astype(BF16), w_proj=w_ple_proj.astype(BF16),
    )


def _row(v):
    return v.reshape(1, -1)


def _channel_tail(x1, h2, p, w, g_ple, g_final):
    u0, s1, tau, invz = _peer_route(h2, w["wqt"], w["sk"])
    ffn = _peer_dense(h2, w["u"], w["vt"], u0, s1, tau, invz)
    return _ple_final(x1, ffn, p, _row(g_ple), w["w_gate"], w["w_proj"], _row(g_final))


def _prompt_group(x, p, rel_bias, w, g_mix, g_q_lat, g_kv_lat, g_ffn, g_ple, g_final):
    batch, seq, d = x.shape
    t = batch * seq
    xf = x.reshape(t, d)
    z = _norm_matmul(xf, _row(g_mix), w["w_in"], Z_TILE)
    cos, sin = _rope_tables(jnp.tile(jnp.arange(seq), batch))
    ckv, krope, kc, kr, qa, qr = _mla_prep(z, cos, sin, _row(g_q_lat), w["w_uq"], _row(g_kv_lat), w["w_ukt"],
                                           w["sela"], w["selb"])
    oa = _mla_prompt_attn(qa, qr, kc, kr, w["w_uv"], batch, seq)

    nblk = seq // MOBA_BLOCK
    kmean, km, vm = _moba_kprep(z, seq)
    kmean = jnp.pad(kmean.reshape(batch, nblk, -1), ((0, 0), (0, LANES - nblk), (0, 0)))
    qm = _moba_select(z, kmean, seq)
    r = jnp.arange(MOBA_BLOCK)[:, None]
    c = jnp.arange(2 * MOBA_BLOCK)[None, :]
    tb = _bias_table(rel_bias, _t5_bucket(r - c + MOBA_BLOCK))
    ob = _moba_prompt_attn(rel_bias, qm, km, vm, tb, batch, seq)

    x1, h2 = _merge(xf, oa, ob, z, w["w_a"], w["w_b"], w["w_o"], _row(g_ffn))
    y = _channel_tail(x1, h2, p.reshape(t, -1), w, g_ple, g_final)
    mk = z[:, OFF_MK:OFF_MV].reshape(batch, seq, MOBA_KV_HEADS, MOBA_DIM)
    mv = z[:, OFF_MV:OFF_R].reshape(batch, seq, MOBA_KV_HEADS, MOBA_DIM)
    return (y.reshape(batch, seq, d), ckv.reshape(batch, seq, -1), krope.reshape(batch, seq, -1), mk, mv)


def _head_major(a, nseq, ds):
    h, _, width = a.shape
    return jnp.transpose(a.reshape(h, nseq, ds, width), (1, 0, 2, 3)).reshape(nseq, h * ds, width)


def _pad_new(a, nseq, ds):
    return jnp.pad(a.reshape(nseq, ds, -1), ((0, 0), (0, NEW_PAD - ds), (0, 0)))


def _sample_group(x, p, page_table, caches, rel_bias, w, g_mix, g_q_lat, g_kv_lat, g_ffn, g_ple, g_final):
    cache_ckv, cache_krope, cache_k, cache_v = caches
    nseq, ds, d = x.shape
    t = nseq * ds
    npages = page_table.shape[1]
    past = npages * PAGE
    xf = x.reshape(t, d)
    z = _norm_matmul(xf, _row(g_mix), w["w_in"], Z_TILE)
    cos, sin = _rope_tables(jnp.tile(past + jnp.arange(ds), nseq))
    ckv, krope, kc, kr, qa, qr = _mla_prep(z, cos, sin, _row(g_q_lat), w["w_uq"], _row(g_kv_lat), w["w_ukt"],
                                           w["sela"], w["selb"])
    o_lat = _mla_sample_attn(page_table, _head_major(qa, nseq, ds), _head_major(qr[:, :, :MLA_ROPE], nseq, ds),
                             _pad_new(kc, nseq, ds), _pad_new(kr[:, :MLA_ROPE], nseq, ds),
                             cache_ckv, cache_krope, ds)
    o_lat = jnp.transpose(o_lat.reshape(nseq, MLA_HEADS, ds, -1), (1, 0, 2, 3)).reshape(MLA_HEADS, t, -1)
    oa = _head_proj(o_lat.astype(BF16), w["w_uv"])

    kvw = MOBA_KV_HEADS * MOBA_DIM
    n_pool = cache_k.shape[1]
    ck = cache_k.reshape(1, n_pool, PAGE, kvw)
    cv = cache_v.reshape(1, n_pool, PAGE, kvw)
    mq = z[:, OFF_MQ:OFF_MK].astype(BF16).reshape(t, MOBA_HEADS, MOBA_DIM)
    mq = _head_major(jnp.transpose(mq, (1, 0, 2)), nseq, ds)
    mask = _moba_sample_select(page_table, mq, ck)

    keys = MOBA_PAGES_PER_STEP * PAGE
    nsteps = npages // MOBA_PAGES_PER_STEP
    blk_of_key = (np.arange(nsteps)[:, None] * keys + np.arange(keys)[None, :]) // MOBA_BLOCK
    expand = jnp.asarray(np.arange(LANES)[None, :, None] == blk_of_key[:, None, :], BF16)
    qpos = past + jnp.arange(ds)[:, None]
    far = jnp.full((ds, keys), REL_BUCKETS - 1, jnp.int32)
    near = _t5_bucket(qpos - ((nsteps - 1) * keys + jnp.arange(keys))[None, :])
    own = _t5_bucket(jnp.arange(ds)[:, None] - jnp.arange(NEW_PAD)[None, :])
    bias = _bias_table(rel_bias, jnp.concatenate([far, near], axis=0))
    bias = jnp.transpose(bias.reshape(MOBA_HEADS, 2, ds, keys), (1, 0, 2, 3)).reshape(2, MOBA_HEADS * ds, keys)
    bown = _bias_table(rel_bias, own).reshape(MOBA_HEADS * ds, NEW_PAD)
    kn = _pad_new(z[:, OFF_MK:OFF_MV].astype(BF16), nseq, ds)
    vn = _pad_new(z[:, OFF_MV:OFF_R].astype(BF16), nseq, ds)
    o_m = _moba_sample_attn(page_table, mq, mask, expand, bias, kn, vn, bown, ck, cv, ds)
    ob = jnp.transpose(o_m.reshape(nseq, MOBA_HEADS, ds, MOBA_DIM), (0, 2, 1, 3)).reshape(t, -1).astype(BF16)

    x1, h2 = _merge(xf, oa, ob, z, w["w_a"], w["w_b"], w["w_o"], _row(g_ffn))
    y = _channel_tail(x1, h2, p.reshape(t, -1), w, g_ple, g_final)
    mk = z[:, OFF_MK:OFF_MV].reshape(nseq, ds, MOBA_KV_HEADS, MOBA_DIM)
    mv = z[:, OFF_MV:OFF_R].reshape(nseq, ds, MOBA_KV_HEADS, MOBA_DIM)
    return (y.reshape(nseq, ds, d), ckv.reshape(nseq, ds, -1), krope.reshape(nseq, ds, -1), mk, mv)


def kernel(x_prompt, x_sample, cache_mla_ckv, cache_mla_krope, cache_moba_k, cache_moba_v, page_table,
           p_prompt, p_sample, rel_bias, g_mix, w_in, g_q_lat, w_uq, g_kv_lat, w_uk, w_uv, w_a_out,
           w_b_out, w_o, g_ffn, peer_wq, peer_subkeys, peer_u, peer_v, g_ple, w_ple_gate, w_ple_proj, g_final):
    assert g_mix.shape[0] == 1, "single-layer step"
    w = _prepare_weights(w_in[0], w_uq[0], w_uk[0], w_uv[0], w_a_out[0], w_b_out[0], w_o[0], peer_wq[0],
                         peer_subkeys[0], peer_u[0], peer_v[0], w_ple_gate[0], w_ple_proj[0])
    gains = (g_mix[0], g_q_lat[0], g_kv_lat[0], g_ffn[0], g_ple[0], g_final)
    yp, ckv_p, kr_p, k_p, v_p = _prompt_group(x_prompt, p_prompt[0], rel_bias, w, *gains)
    caches = (cache_mla_ckv, cache_mla_krope, cache_moba_k, cache_moba_v)
    ys, ckv_s, kr_s, k_s, v_s = _sample_group(x_sample, p_sample[0], page_table, caches, rel_bias, w, *gains)
    return (yp, ys, ckv_p[None], kr_p[None], k_p[None], v_p[None], ckv_s[None], kr_s[None], k_s[None], v_s[None])
```

```python
import functools
import math

import numpy as np
import jax
import jax.numpy as jnp
from jax import lax
from jax.experimental import pallas as pl
from jax.experimental.pallas import tpu as pltpu

F32 = jnp.float32
BF16 = jnp.bfloat16

D_MODEL = 2048
MLA_HEADS = 8
MLA_LORA = 512
MLA_NOPE = 128
MLA_ROPE = 64
MLA_V = 128
ROPE_THETA = 10000.0
MOBA_HEADS = 8
MOBA_KV_HEADS = 2
MOBA_GROUP = MOBA_HEADS // MOBA_KV_HEADS
MOBA_DIM = 128
MOBA_BLOCK = 256
MOBA_TOPK = 3
REL_BUCKETS = 32
REL_MAX_DIST = 128
PEER_HEADS = 8
PEER_NKEYS = 128
PEER_TOPK = 16
PLE_DIM = 256
PAGE = 128
NORM_EPS = 1e-6
NEG_INF = -1e30
MASK_BIG = -(2.0 ** 100)

LANES = 128
VMEM_LIMIT = 56 * 1024 * 1024

OFF_A = 0
OFF_B = D_MODEL
OFF_Q = 2 * D_MODEL
OFF_C = OFF_Q + MLA_LORA
OFF_MQ = OFF_C + MLA_LORA
OFF_MK = OFF_MQ + MOBA_HEADS * MOBA_DIM
OFF_MV = OFF_MK + MOBA_KV_HEADS * MOBA_DIM
OFF_R = OFF_MV + MOBA_KV_HEADS * MOBA_DIM
Z_WIDTH = 6912
Z_TILE = 1152


def _cparams(sem):
    return pltpu.CompilerParams(dimension_semantics=sem, vmem_limit_bytes=VMEM_LIMIT)


def _resident(shape):
    nd = len(shape)
    return pl.BlockSpec(shape, lambda *_: (0,) * nd, pipeline_mode=pl.Buffered(1))


def _dot(a, b):
    return jnp.dot(a, b, preferred_element_type=F32)


def _dot_nt(a, b):
    return lax.dot_general(a, b, (((1,), (1,)), ((), ())), preferred_element_type=F32)


def _rms(x, g):
    return x * lax.rsqrt(jnp.mean(x * x, axis=-1, keepdims=True) + NORM_EPS) * g


def _sigmoid(x):
    return 1.0 / (1.0 + jnp.exp(-x))


def _norm_matmul_kernel(x_ref, g_ref, w_ref, o_ref, h_ref):
    @pl.when(pl.program_id(1) == 0)
    def _():
        h_ref[...] = _rms(x_ref[...], g_ref[...]).astype(BF16)

    o_ref[...] = _dot(h_ref[...], w_ref[...])


def _norm_matmul(x, g, w, tn):
    m, k = x.shape
    n = w.shape[1]
    tm = min(512, m)
    return pl.pallas_call(
        _norm_matmul_kernel,
        grid=(m // tm, n // tn),
        in_specs=[pl.BlockSpec((tm, k), lambda i, j: (i, 0)),
                  pl.BlockSpec((1, k), lambda i, j: (0, 0)),
                  pl.BlockSpec((k, tn), lambda i, j: (0, j))],
        out_specs=pl.BlockSpec((tm, tn), lambda i, j: (i, j)),
        out_shape=jax.ShapeDtypeStruct((m, n), F32),
        scratch_shapes=[pltpu.VMEM((tm, k), BF16)],
        compiler_params=_cparams(("parallel", "arbitrary")),
        name="norm_matmul",
    )(x, g, w)


def _mla_prep_kernel(zq_ref, zc_ref, zr_ref, cos_ref, sin_ref, gq_ref, wuq_ref, gkv_ref, wukt_ref,
                     sela_ref, selb_ref, ckv_ref, krope_ref, kc_ref, kr_ref, qa_ref, qr_ref):
    nope = MLA_HEADS * MLA_NOPE
    half = MLA_HEADS * MLA_ROPE // 2
    q_all = _dot(_rms(zq_ref[...], gq_ref[...]).astype(BF16), wuq_ref[...])
    cos = cos_ref[...]
    sin = sin_ref[...]
    x1 = q_all[:, nope:nope + half]
    x2 = q_all[:, nope + half:]
    r1 = (x1 * cos - x2 * sin).astype(BF16)
    r2 = (x2 * cos + x1 * sin).astype(BF16)
    q_rope = _dot(r1, sela_ref[...]) + _dot(r2, selb_ref[...])
    for h in range(MLA_HEADS):
        qr_ref[h] = q_rope[:, h * LANES:(h + 1) * LANES].astype(BF16)
        q_nope = q_all[:, h * MLA_NOPE:(h + 1) * MLA_NOPE].astype(BF16)
        qa_ref[h] = _dot(q_nope, wukt_ref[h]).astype(BF16)

    ckv = _rms(zc_ref[...], gkv_ref[...])
    ckv_ref[...] = ckv
    kc_ref[...] = ckv.astype(BF16)

    zr = zr_ref[...]
    k1 = zr[:, :MLA_ROPE // 2]
    k2 = zr[:, MLA_ROPE // 2:MLA_ROPE]
    ck = cos[:, :MLA_ROPE // 2]
    sk = sin[:, :MLA_ROPE // 2]
    kr = jnp.concatenate([k1 * ck - k2 * sk, k2 * ck + k1 * sk], axis=1)
    krope_ref[...] = kr
    kr_ref[...] = jnp.concatenate([kr, jnp.zeros_like(kr)], axis=1).astype(BF16)


def _mla_prep(z, cos, sin, gq, wuq, gkv, wukt, sela, selb):
    t = z.shape[0]
    tm = min(256, t)
    row = lambda c: (lambda i: (i, c))
    hrow = lambda i: (0, i, 0)
    return pl.pallas_call(
        _mla_prep_kernel,
        grid=(t // tm,),
        in_specs=[pl.BlockSpec((tm, MLA_LORA), row(OFF_Q // MLA_LORA)),
                  pl.BlockSpec((tm, MLA_LORA), row(OFF_C // MLA_LORA)),
                  pl.BlockSpec((tm, LANES), row(OFF_R // LANES)),
                  pl.BlockSpec((tm, 256), row(0)),
                  pl.BlockSpec((tm, 256), row(0)),
                  _resident(gq.shape), _resident(wuq.shape), _resident(gkv.shape),
                  _resident(wukt.shape), _resident(sela.shape), _resident(selb.shape)],
        out_specs=[pl.BlockSpec((tm, MLA_LORA), row(0)),
                   pl.BlockSpec((tm, MLA_ROPE), row(0)),
                   pl.BlockSpec((tm, MLA_LORA), row(0)),
                   pl.BlockSpec((tm, LANES), row(0)),
                   pl.BlockSpec((MLA_HEADS, tm, MLA_LORA), hrow),
                   pl.BlockSpec((MLA_HEADS, tm, LANES), hrow)],
        out_shape=[jax.ShapeDtypeStruct((t, MLA_LORA), F32),
                   jax.ShapeDtypeStruct((t, MLA_ROPE), F32),
                   jax.ShapeDtypeStruct((t, MLA_LORA), BF16),
                   jax.ShapeDtypeStruct((t, LANES), BF16),
                   jax.ShapeDtypeStruct((MLA_HEADS, t, MLA_LORA), BF16),
                   jax.ShapeDtypeStruct((MLA_HEADS, t, LANES), BF16)],
        compiler_params=_cparams(("parallel",)),
        name="mla_prep",
    )(z, z, z, cos, sin, gq, wuq, gkv, wukt, sela, selb)


ATTN_SPLIT = 2


def _softmax_step(s, m_ref, l_ref, acc_ref, pv):
    m_prev = m_ref[...]
    m_new = jnp.maximum(m_prev, jnp.max(s, axis=-1, keepdims=True))
    alpha = jnp.exp(m_prev - m_new)
    p = jnp.exp(s - m_new)
    l_ref[...] = alpha * l_ref[...] + jnp.sum(p, axis=-1, keepdims=True)
    acc_ref[...] = alpha * acc_ref[...] + pv(p.astype(BF16))
    m_ref[...] = m_new


def _softmax_init(m_ref, l_ref, acc_ref):
    m_ref[...] = jnp.full(m_ref.shape, -jnp.inf, F32)
    l_ref[...] = jnp.zeros(l_ref.shape, F32)
    acc_ref[...] = jnp.zeros(acc_ref.shape, F32)


def _mla_attn_kernel(qa_ref, qr_ref, kc_ref, kr_ref, wuv_ref, o_ref, m_ref, l_ref, acc_ref, *, tq, tk):
    i = pl.program_id(1)
    j = pl.program_id(2)
    last = (i * tq + tq - 1) // tk
    rows = MLA_HEADS * tq
    scale = (MLA_NOPE + MLA_ROPE) ** -0.5

    @pl.when(j == 0)
    def _():
        _softmax_init(m_ref, l_ref, acc_ref)

    def pv(p):
        return _dot(p, kc_ref[...])

    def step(causal):
        hh = MLA_HEADS // ATTN_SPLIT
        nr = hh * tq
        ss = []
        for part in range(ATTN_SPLIT):
            qa = qa_ref[part * hh:(part + 1) * hh].reshape(nr, MLA_LORA)
            qr = qr_ref[part * hh:(part + 1) * hh].reshape(nr, LANES)
            ss.append((_dot_nt(qa, kc_ref[...]) + _dot_nt(qr, kr_ref[...])) * scale)
        for part in range(ATTN_SPLIT):
            s = ss[part]
            if causal:
                qpos = i * tq + (lax.broadcasted_iota(jnp.int32, s.shape, 0) & (tq - 1))
                kpos = j * tk + lax.broadcasted_iota(jnp.int32, s.shape, 1)
                s = jnp.where(kpos <= qpos, s, NEG_INF)
            sl = slice(part * nr, (part + 1) * nr)
            _softmax_step(s, m_ref.at[sl], l_ref.at[sl], acc_ref.at[sl], pv)

    @pl.when(j < last)
    def _():
        step(False)

    @pl.when(j == last)
    def _():
        step(True)
        o_lat = (acc_ref[...] / l_ref[...]).astype(BF16)
        for h in range(MLA_HEADS):
            o_ref[:, h * MLA_V:(h + 1) * MLA_V] = _dot(o_lat[h * tq:(h + 1) * tq], wuv_ref[h]).astype(BF16)


def _mla_prompt_attn(qa, qr, kc, kr, wuv, batch, seq):
    tq = min(256, seq)
    tk = min(512, seq)
    nq, nk = seq // tq, seq // tk
    t = batch * seq

    def qmap(b, i, j):
        return (0, b * nq + i, 0)

    def kmap(b, i, j):
        return (b * nk + jnp.minimum(j, (i * tq + tq - 1) // tk), 0)

    rows = MLA_HEADS * tq
    return pl.pallas_call(
        functools.partial(_mla_attn_kernel, tq=tq, tk=tk),
        grid=(batch, nq, nk),
        in_specs=[pl.BlockSpec((MLA_HEADS, tq, MLA_LORA), qmap),
                  pl.BlockSpec((MLA_HEADS, tq, LANES), qmap),
                  pl.BlockSpec((tk, MLA_LORA), kmap),
                  pl.BlockSpec((tk, LANES), kmap),
                  _resident(wuv.shape)],
        out_specs=pl.BlockSpec((tq, MLA_HEADS * MLA_V), lambda b, i, j: (b * nq + i, 0)),
        out_shape=jax.ShapeDtypeStruct((t, MLA_HEADS * MLA_V), BF16),
        scratch_shapes=[pltpu.VMEM((rows, 1), F32), pltpu.VMEM((rows, 1), F32),
                        pltpu.VMEM((rows, MLA_LORA), F32)],
        compiler_params=_cparams(("parallel", "parallel", "arbitrary")),
        name="mla_prompt_attn",
    )(qa, qr, kc, kr, wuv)


def _moba_kprep_kernel(zk_ref, zv_ref, mean_ref, km_ref, vm_ref, *, nblk):
    n = pl.program_id(0) % nblk
    k = zk_ref[...]
    v = zv_ref[...]
    mean_ref[0] = jnp.mean(k, axis=0, keepdims=True)
    onehot = jnp.where(lax.broadcasted_iota(jnp.int32, (MOBA_BLOCK, LANES), 1) == n, 1.0, 0.0).astype(BF16)
    for g in range(MOBA_KV_HEADS):
        km_ref[g] = jnp.concatenate([k[:, g * MOBA_DIM:(g + 1) * MOBA_DIM].astype(BF16), onehot], axis=1)
        vm_ref[g] = v[:, g * MOBA_DIM:(g + 1) * MOBA_DIM].astype(BF16)


def _moba_kprep(z, seq):
    t = z.shape[0]
    nb = t // MOBA_BLOCK
    kvw = MOBA_KV_HEADS * MOBA_DIM
    return pl.pallas_call(
        functools.partial(_moba_kprep_kernel, nblk=seq // MOBA_BLOCK),
        grid=(nb,),
        in_specs=[pl.BlockSpec((MOBA_BLOCK, kvw), lambda i: (i, OFF_MK // kvw)),
                  pl.BlockSpec((MOBA_BLOCK, kvw), lambda i: (i, OFF_MV // kvw))],
        out_specs=[pl.BlockSpec((1, 1, kvw), lambda i: (i, 0, 0)),
                   pl.BlockSpec((MOBA_KV_HEADS, MOBA_BLOCK, 2 * MOBA_DIM), lambda i: (0, i, 0)),
                   pl.BlockSpec((MOBA_KV_HEADS, MOBA_BLOCK, MOBA_DIM), lambda i: (0, i, 0))],
        out_shape=[jax.ShapeDtypeStruct((nb, 1, kvw), F32),
                   jax.ShapeDtypeStruct((MOBA_KV_HEADS, t, 2 * MOBA_DIM), BF16),
                   jax.ShapeDtypeStruct((MOBA_KV_HEADS, t, MOBA_DIM), BF16)],
        compiler_params=_cparams(("parallel",)),
        name="moba_kprep",
    )(z, z)


def _top_mask(s, k):
    lane = lax.broadcasted_iota(jnp.int32, s.shape, 1)
    sel = jnp.zeros(s.shape, jnp.bool_)
    for _ in range(k):
        m = jnp.max(s, axis=-1, keepdims=True)
        idx = jnp.min(jnp.where(s == m, lane, s.shape[1]), axis=-1, keepdims=True)
        hit = lane == idx
        sel = jnp.logical_or(sel, hit)
        s = jnp.where(hit, -jnp.inf, s)
    return sel


def _moba_select_kernel(zq_ref, km_ref, qm_ref, *, nblk):
    own = pl.program_id(0) % nblk
    zq = zq_ref[...]
    km = km_ref[...].astype(BF16)
    for h in range(MOBA_HEADS):
        g = h // MOBA_GROUP
        q = zq[:, h * MOBA_DIM:(h + 1) * MOBA_DIM].astype(BF16)
        s = _dot_nt(q, km[:, g * MOBA_DIM:(g + 1) * MOBA_DIM])
        lane = lax.broadcasted_iota(jnp.int32, s.shape, 1)
        past = lane < own
        sel = _top_mask(jnp.where(past, s, NEG_INF), MOBA_TOPK)
        mask = jnp.where(jnp.logical_and(past, jnp.logical_not(sel)), MASK_BIG, 0.0)
        qm_ref[h] = jnp.concatenate([q, mask.astype(BF16)], axis=1)


def _moba_select(z, kmean, seq):
    t = z.shape[0]
    nblk = seq // MOBA_BLOCK
    qw = MOBA_HEADS * MOBA_DIM
    return pl.pallas_call(
        functools.partial(_moba_select_kernel, nblk=nblk),
        grid=(t // MOBA_BLOCK,),
        in_specs=[pl.BlockSpec((MOBA_BLOCK, qw), lambda i: (i, OFF_MQ // qw)),
                  pl.BlockSpec((None, LANES, MOBA_KV_HEADS * MOBA_DIM), lambda i: (i // nblk, 0, 0))],
        out_specs=pl.BlockSpec((MOBA_HEADS, MOBA_BLOCK, 2 * MOBA_DIM), lambda i: (0, i, 0)),
        out_shape=jax.ShapeDtypeStruct((MOBA_HEADS, t, 2 * MOBA_DIM), BF16),
        compiler_params=_cparams(("parallel",)),
        name="moba_select",
    )(z, kmean)


BIAS_MASKED = REL_BUCKETS


def _bias_table_kernel(rb_ref, bucket_ref, o_ref):
    bucket = bucket_ref[...]
    for h in range(MOBA_HEADS):
        acc = jnp.where(bucket == BIAS_MASKED, NEG_INF, 0.0)
        for b in range(REL_BUCKETS):
            acc = jnp.where(bucket == b, rb_ref[b, h], acc)
        o_ref[h] = acc


def _bias_table(rel_bias, bucket):
    return pl.pallas_call(
        _bias_table_kernel,
        in_specs=[pl.BlockSpec(memory_space=pltpu.SMEM), pl.BlockSpec(memory_space=pltpu.VMEM)],
        out_specs=pl.BlockSpec(memory_space=pltpu.VMEM),
        out_shape=jax.ShapeDtypeStruct((MOBA_HEADS,) + bucket.shape, F32),
        name="bias_table",
    )(rel_bias, bucket)


MOBA_CHUNK_BLOCKS = 4
MOBA_TILE_KINDS = 4


def _moba_tile_buckets():
    r = jnp.arange(MOBA_BLOCK)[:, None]
    c = jnp.arange(MOBA_BLOCK)[None, :]
    far = jnp.full((MOBA_BLOCK, MOBA_BLOCK), REL_BUCKETS - 1, jnp.int32)
    prev = _t5_bucket(r - c + MOBA_BLOCK)
    own = jnp.where(c <= r, _t5_bucket(r - c), BIAS_MASKED)
    future = jnp.full((MOBA_BLOCK, MOBA_BLOCK), BIAS_MASKED, jnp.int32)
    return jnp.concatenate([far, prev, own, future], axis=0)


def _moba_attn_kernel(qm_ref, km_ref, vm_ref, tb_ref, o_ref, m_ref, l_ref, acc_ref, *, cb):
    i = pl.program_id(2)
    c = pl.program_id(3)
    rows = MOBA_GROUP * MOBA_BLOCK
    scale = MOBA_DIM ** -0.5

    @pl.when(c == 0)
    def _():
        _softmax_init(m_ref, l_ref, acc_ref)

    @pl.when(c * cb <= i)
    def _():
        hh = MOBA_GROUP // ATTN_SPLIT
        nr = hh * MOBA_BLOCK
        kinds = [jnp.clip(2 - (i - (c * cb + p)), 0, MOBA_TILE_KINDS - 1) for p in range(cb)]
        ss = [_dot_nt(qm_ref[part * hh:(part + 1) * hh].reshape(nr, 2 * MOBA_DIM), km_ref[0]) * scale
              for part in range(ATTN_SPLIT)]
        for part in range(ATTN_SPLIT):
            s = ss[part].reshape(hh, MOBA_BLOCK, cb * MOBA_BLOCK)
            s = jnp.concatenate([s[:, :, p * MOBA_BLOCK:(p + 1) * MOBA_BLOCK]
                                 + tb_ref[kinds[p], part * hh:(part + 1) * hh] for p in range(cb)], axis=2)
            sl = slice(part * nr, (part + 1) * nr)
            _softmax_step(s.reshape(nr, cb * MOBA_BLOCK), m_ref.at[sl], l_ref.at[sl], acc_ref.at[sl],
                          lambda pr: _dot(pr, vm_ref[0]))

    @pl.when(c == pl.num_programs(3) - 1)
    def _():
        o = acc_ref[...] / l_ref[...]
        for hh in range(MOBA_GROUP):
            o_ref[:, hh * MOBA_DIM:(hh + 1) * MOBA_DIM] = o[hh * MOBA_BLOCK:(hh + 1) * MOBA_BLOCK].astype(BF16)


def _moba_prompt_attn(qm, km, vm, tb, batch, seq):
    nblk = seq // MOBA_BLOCK
    cb = min(MOBA_CHUNK_BLOCKS, nblk)
    nchunk = nblk // cb
    t = batch * seq
    rows = MOBA_GROUP * MOBA_BLOCK

    def kmap(b, g, i, c):
        return (g, b * nchunk + jnp.minimum(c, i // cb), 0)

    return pl.pallas_call(
        functools.partial(_moba_attn_kernel, cb=cb),
        grid=(batch, MOBA_KV_HEADS, nblk, nchunk),
        in_specs=[pl.BlockSpec((MOBA_GROUP, MOBA_BLOCK, 2 * MOBA_DIM), lambda b, g, i, c: (g, b * nblk + i, 0)),
                  pl.BlockSpec((1, cb * MOBA_BLOCK, 2 * MOBA_DIM), kmap),
                  pl.BlockSpec((1, cb * MOBA_BLOCK, MOBA_DIM), kmap),
                  pl.BlockSpec((MOBA_TILE_KINDS, MOBA_GROUP, MOBA_BLOCK, MOBA_BLOCK), lambda b, g, i, c: (0, g, 0, 0))],
        out_specs=pl.BlockSpec((MOBA_BLOCK, MOBA_GROUP * MOBA_DIM), lambda b, g, i, c: (b * nblk + i, g)),
        out_shape=jax.ShapeDtypeStruct((t, MOBA_HEADS * MOBA_DIM), BF16),
        scratch_shapes=[pltpu.VMEM((rows, 1), F32), pltpu.VMEM((rows, 1), F32),
                        pltpu.VMEM((rows, MOBA_DIM), F32)],
        compiler_params=_cparams(("parallel", "parallel", "parallel", "arbitrary")),
        name="moba_prompt_attn",
    )(qm, km, vm, tb)


def _merge_kernel(x_ref, oa_ref, ob_ref, za_ref, zb_ref, wa_ref, wb_ref, wo_ref, g_ref, x1_ref, h_ref):
    a = _dot(oa_ref[...], wa_ref[...])
    b = _dot(ob_ref[...], wb_ref[...])
    y = _sigmoid(za_ref[...]) * a + _sigmoid(zb_ref[...]) * b
    x1 = x_ref[...] + _dot(y.astype(BF16), wo_ref[...])
    x1_ref[...] = x1
    h_ref[...] = _rms(x1, g_ref[...]).astype(BF16)


def _merge(x, oa, ob, z, wa, wb, wo, g):
    t, d = x.shape
    tm = min(256, t)
    row = lambda c: (lambda i: (i, c))
    return pl.pallas_call(
        _merge_kernel,
        grid=(t // tm,),
        in_specs=[pl.BlockSpec((tm, d), row(0)),
                  pl.BlockSpec((tm, oa.shape[1]), row(0)),
                  pl.BlockSpec((tm, ob.shape[1]), row(0)),
                  pl.BlockSpec((tm, d), row(OFF_A // d)),
                  pl.BlockSpec((tm, d), row(OFF_B // d)),
                  _resident(wa.shape), _resident(wb.shape), _resident(wo.shape), _resident(g.shape)],
        out_specs=[pl.BlockSpec((tm, d), row(0)), pl.BlockSpec((tm, d), row(0))],
        out_shape=[jax.ShapeDtypeStruct((t, d), F32), jax.ShapeDtypeStruct((t, d), BF16)],
        compiler_params=_cparams(("parallel",)),
        name="mixer_merge",
    )(x, oa, ob, z, z, wa, wb, wo, g)


def _top_rows(s, k):
    row = lax.broadcasted_iota(jnp.int32, s.shape, 0)
    rest = s
    vals = []
    for _ in range(k):
        m = jnp.max(rest, axis=0, keepdims=True)
        idx = jnp.min(jnp.where(rest == m, row, s.shape[0]), axis=0, keepdims=True)
        rest = jnp.where(row == idx, -jnp.inf, rest)
        vals.append(m)
    return jnp.concatenate(vals, axis=0), jnp.where(rest == -jnp.inf, s, -jnp.inf)


def _peer_route_kernel(h_ref, wqt_ref, sk_ref, u0_ref, s1_ref, tau_ref, invz_ref):
    qt = _dot_nt(wqt_ref[...], h_ref[...])
    nk = PEER_NKEYS
    for p in range(PEER_HEADS):
        halves = []
        for half in range(2):
            g = 2 * p + half
            s = _dot(sk_ref[g], qt[g * nk:(g + 1) * nk].astype(BF16))
            halves.append(_top_rows(s, PEER_TOPK))
        (v0, s0), (v1, s1) = halves
        top = v0[0:1] + v1[0:1]
        u0 = s0 - top
        v0s = v0 - top
        hk = PEER_TOPK // 2
        cand = jnp.concatenate([v0s[a:a + 1] + v1[:hk] for a in range(hk)]
                               + [v0s[0:1] + v1[hk:], v0s[hk:] + v1[0:1]], axis=0)
        tau = _top_rows(cand, PEER_TOPK)[0][PEER_TOPK - 1:PEER_TOPK]
        z = jnp.sum(jnp.where(cand >= tau, jnp.exp(cand), 0.0), axis=0, keepdims=True)
        u0_ref[p] = u0
        s1_ref[p] = s1
        tau_ref[p:p + 1, :] = tau
        invz_ref[p:p + 1, :] = 1.0 / z


def _peer_route(h, wqt, sk):
    t, d = h.shape
    tt = min(256, t)
    nk = PEER_NKEYS
    tok = lambda i: (0, 0, i)
    return pl.pallas_call(
        _peer_route_kernel,
        grid=(t // tt,),
        in_specs=[pl.BlockSpec((tt, d), lambda i: (i, 0)), _resident(wqt.shape), _resident(sk.shape)],
        out_specs=[pl.BlockSpec((PEER_HEADS, nk, tt), tok), pl.BlockSpec((PEER_HEADS, nk, tt), tok),
                   pl.BlockSpec((PEER_HEADS, tt), lambda i: (0, i)),
                   pl.BlockSpec((PEER_HEADS, tt), lambda i: (0, i))],
        out_shape=[jax.ShapeDtypeStruct((PEER_HEADS, nk, t), F32), jax.ShapeDtypeStruct((PEER_HEADS, nk, t), F32),
                   jax.ShapeDtypeStruct((PEER_HEADS, t), F32), jax.ShapeDtypeStruct((PEER_HEADS, t), F32)],
        compiler_params=_cparams(("parallel",)),
        name="peer_route",
    )(h, wqt, sk)


PEER_EXPERT_TILE = 1024
PEER_ROWS_PER_TILE = PEER_EXPERT_TILE // PEER_NKEYS
PEER_SUB_ROWS = 2


def _peer_dense_kernel(h_ref, u_ref, vt_ref, u0_ref, s1_ref, tau_ref, invz_ref, o_ref, acc_ref):
    e = pl.program_id(1)

    @pl.when(e == 0)
    def _():
        acc_ref[...] = jnp.zeros(acc_ref.shape, F32)

    nk = PEER_NKEYS
    sub = PEER_SUB_ROWS * nk
    nsub = PEER_EXPERT_TILE // sub
    h = h_ref[...]
    half = PEER_EXPERT_TILE // 2
    halves = [_dot_nt(u_ref[k * half:(k + 1) * half, :], h) for k in range(2)]
    per_half = half // sub
    out = None
    for k in range(nsub):
        at = halves[k // per_half][(k % per_half) * sub:(k % per_half + 1) * sub]
        act = 0.5 * at * (1.0 + lax.erf(at * (2.0 ** -0.5)))
        ws = []
        for r in range(PEER_SUB_ROWS):
            ii = k * PEER_SUB_ROWS + r
            gate = jnp.zeros((nk, at.shape[1]), F32)
            for p in range(PEER_HEADS):
                val = u0_ref[p, ii:ii + 1, :] + s1_ref[p]
                gate = gate + jnp.where(val >= tau_ref[p:p + 1, :], jnp.exp(val) * invz_ref[p:p + 1, :], 0.0)
            ws.append((gate * act[r * nk:(r + 1) * nk]).astype(BF16))
        part = _dot(vt_ref[:, k * sub:(k + 1) * sub], jnp.concatenate(ws, axis=0))
        out = part if out is None else out + part
    acc_ref[...] += out

    @pl.when(e == pl.num_programs(1) - 1)
    def _():
        o_ref[...] = acc_ref[...].T


def _peer_dense(h, u, vt, u0, s1, tau, invz):
    t, d = h.shape
    tt = min(512, t)
    te = PEER_EXPERT_TILE
    n_exp = u.shape[0]
    nk = PEER_NKEYS
    return pl.pallas_call(
        _peer_dense_kernel,
        grid=(t // tt, n_exp // te),
        in_specs=[pl.BlockSpec((tt, d), lambda i, e: (i, 0)),
                  pl.BlockSpec((te, d), lambda i, e: (e, 0)),
                  pl.BlockSpec((d, te), lambda i, e: (0, e)),
                  pl.BlockSpec((PEER_HEADS, PEER_ROWS_PER_TILE, tt), lambda i, e: (0, e, i)),
                  pl.BlockSpec((PEER_HEADS, nk, tt), lambda i, e: (0, 0, i)),
                  pl.BlockSpec((PEER_HEADS, tt), lambda i, e: (0, i)),
                  pl.BlockSpec((PEER_HEADS, tt), lambda i, e: (0, i))],
        out_specs=pl.BlockSpec((tt, d), lambda i, e: (i, 0)),
        out_shape=jax.ShapeDtypeStruct((t, d), F32),
        scratch_shapes=[pltpu.VMEM((d, tt), F32)],
        compiler_params=_cparams(("parallel", "arbitrary")),
        name="peer_dense",
    )(h, u, vt, u0, s1, tau, invz)


def _ple_final_kernel(x_ref, f_ref, p_ref, gp_ref, wg_ref, wp_ref, gf_ref, o_ref):
    x = x_ref[...] + f_ref[...]
    gate = _sigmoid(_dot(_rms(x, gp_ref[...]).astype(BF16), wg_ref[...]))
    x = x + gate * _dot(p_ref[...].astype(BF16), wp_ref[...])
    o_ref[...] = _rms(x, gf_ref[...])


def _ple_final(x, ffn, p, gp, wg, wp, gf):
    t, d = x.shape
    tm = min(256, t)
    return pl.pallas_call(
        _ple_final_kernel,
        grid=(t // tm,),
        in_specs=[pl.BlockSpec((tm, d), lambda i: (i, 0)),
                  pl.BlockSpec((tm, d), lambda i: (i, 0)),
                  pl.BlockSpec((tm, p.shape[1]), lambda i: (i, 0)),
                  _resident(gp.shape), _resident(wg.shape), _resident(wp.shape), _resident(gf.shape)],
        out_specs=pl.BlockSpec((tm, d), lambda i: (i, 0)),
        out_shape=jax.ShapeDtypeStruct((t, d), F32),
        compiler_params=_cparams(("parallel",)),
        name="ple_final",
    )(x, ffn, p, gp, wg, wp, gf)


NEW_PAD = LANES
SAMPLE_PAGES_PER_STEP = 32


def _page_spec(rows, width, k, per_step):
    return pl.BlockSpec((None, None, rows, width), lambda b, j, pt: (0, pt[b, j * per_step + k], 0, 0))


def _mla_sample_kernel(pt_ref, qa_ref, qr_ref, kcn_ref, krn_ref, *rest, ds, pg):
    c_refs, r_refs = rest[:pg], rest[pg:2 * pg]
    o_ref, m_ref, l_ref, acc_ref = rest[2 * pg:]
    j = pl.program_id(1)
    scale = (MLA_NOPE + MLA_ROPE) ** -0.5

    @pl.when(j == 0)
    def _():
        _softmax_init(m_ref, l_ref, acc_ref)

    qa = qa_ref[...]
    qr = qr_ref[...]
    cs = [c[...].astype(BF16) for c in c_refs]
    s = jnp.concatenate([_dot_nt(qa, cs[k]) + _dot(qr, r_refs[k][...].astype(BF16)) for k in range(pg)],
                        axis=1) * scale

    def pv(p):
        out = _dot(p[:, :PAGE], cs[0])
        for k in range(1, pg):
            out = out + _dot(p[:, k * PAGE:(k + 1) * PAGE], cs[k])
        return out

    _softmax_step(s, m_ref, l_ref, acc_ref, pv)

    @pl.when(j == pl.num_programs(1) - 1)
    def _():
        kcn = kcn_ref[...]
        sn = (_dot_nt(qa, kcn) + _dot_nt(qr, krn_ref[...])) * scale
        tok = lax.broadcasted_iota(jnp.int32, sn.shape, 0) & (ds - 1)
        col = lax.broadcasted_iota(jnp.int32, sn.shape, 1)
        sn = jnp.where(col <= tok, sn, NEG_INF)
        _softmax_step(sn, m_ref, l_ref, acc_ref, lambda p: _dot(p, kcn))
        o_ref[...] = acc_ref[...] / l_ref[...]


def _mla_sample_attn(page_table, qa, qr, kcn, krn, cache_ckv, cache_krope_t, ds):
    nseq, rows, _ = qa.shape
    pg = min(SAMPLE_PAGES_PER_STEP, page_table.shape[1])
    nsteps = page_table.shape[1] // pg
    seq3 = lambda b, j, pt: (b, 0, 0)
    grid_spec = pltpu.PrefetchScalarGridSpec(
        num_scalar_prefetch=1,
        grid=(nseq, nsteps),
        in_specs=[pl.BlockSpec((None, rows, MLA_LORA), seq3),
                  pl.BlockSpec((None, rows, MLA_ROPE), seq3),
                  pl.BlockSpec((None, NEW_PAD, MLA_LORA), seq3),
                  pl.BlockSpec((None, NEW_PAD, MLA_ROPE), seq3)]
                 + [_page_spec(PAGE, MLA_LORA, k, pg) for k in range(pg)]
                 + [_page_spec(MLA_ROPE, PAGE, k, pg) for k in range(pg)],
        out_specs=pl.BlockSpec((None, rows, MLA_LORA), seq3),
        scratch_shapes=[pltpu.VMEM((rows, 1), F32), pltpu.VMEM((rows, 1), F32), pltpu.VMEM((rows, MLA_LORA), F32)],
    )
    return pl.pallas_call(
        functools.partial(_mla_sample_kernel, ds=ds, pg=pg),
        grid_spec=grid_spec,
        out_shape=jax.ShapeDtypeStruct((nseq, rows, MLA_LORA), F32),
        compiler_params=_cparams(("parallel", "arbitrary")),
        name="mla_sample_attn",
    )(page_table, qa, qr, kcn, krn, *([cache_ckv] * pg), *([cache_krope_t] * pg))


def _head_proj_kernel(o_ref, w_ref, out_ref):
    for h in range(MLA_HEADS):
        out_ref[:, h * MLA_V:(h + 1) * MLA_V] = _dot(o_ref[h], w_ref[h]).astype(BF16)


def _head_proj(o_lat, wuv):
    t = o_lat.shape[1]
    return pl.pallas_call(
        _head_proj_kernel,
        out_shape=jax.ShapeDtypeStruct((t, MLA_HEADS * MLA_V), BF16),
        name="mla_head_proj",
    )(o_lat, wuv)


def _moba_sample_kernel(pt_ref, q_ref, bias_ref, kn_ref, vn_ref, bown_ref, *rest, ds, pg, nblk):
    k_refs, v_refs = rest[:pg], rest[pg:2 * pg]
    o_ref, mean_ref, m_ref, l_ref, part_ref = rest[2 * pg:]
    j = pl.program_id(1)
    last = pl.num_programs(1) - 1
    bps = pg // 2
    scale = MOBA_DIM ** -0.5
    rows = q_ref.shape[0]
    rows_g = rows // MOBA_KV_HEADS

    @pl.when(j == 0)
    def _():
        mean_ref[...] = jnp.zeros(mean_ref.shape, F32)
        m_ref[...] = jnp.zeros(m_ref.shape, F32)
        l_ref[...] = jnp.zeros(l_ref.shape, F32)

    q = q_ref[...]
    lane = lax.broadcasted_iota(jnp.int32, (rows, LANES), 1)
    sub_head = lax.broadcasted_iota(jnp.int32, (8, MOBA_DIM), 0) % MOBA_KV_HEADS
    m_all = m_ref[...]
    l_all = l_ref[...]
    means, scores = [], []
    for n in range(bps):
        ka = k_refs[2 * n][...]
        kb = k_refs[2 * n + 1][...]
        by_sublane = (ka.reshape(-1, 8, MOBA_DIM).sum(axis=0) + kb.reshape(-1, 8, MOBA_DIM).sum(axis=0))
        means.append(jnp.concatenate(
            [jnp.sum(jnp.where(sub_head == g, by_sublane, 0.0), axis=0, keepdims=True) for g in range(MOBA_KV_HEADS)],
            axis=1))
        kk = jnp.concatenate([ka, kb], axis=0).astype(BF16)
        tile = bias_ref[jnp.where(j == last, 1, 0)] if n == bps - 1 else bias_ref[0]
        scores.append(_dot_nt(q, kk) * scale + tile)
    probs = []
    for n in range(bps):
        mb = jnp.max(scores[n], axis=-1, keepdims=True)
        p = jnp.exp(scores[n] - mb)
        hit = lane == j * bps + n
        m_all = jnp.where(hit, mb, m_all)
        l_all = jnp.where(hit, jnp.sum(p, axis=-1, keepdims=True), l_all)
        probs.append(p.astype(BF16))
    for n in range(bps):
        vv = jnp.concatenate([v_refs[2 * n][...], v_refs[2 * n + 1][...]], axis=0).astype(BF16)
        part_ref[j * bps + n] = _dot(probs[n], vv)
    m_ref[...] = m_all
    l_ref[...] = l_all
    mean_ref[pl.ds(pl.multiple_of(j * bps, bps), bps), :] = jnp.concatenate(means, axis=0) * (1.0 / MOBA_BLOCK)

    @pl.when(j == last)
    def _():
        km = mean_ref[...].astype(BF16)

        def group(x, g):
            return x[:, g * MOBA_DIM:(g + 1) * MOBA_DIM]

        def per_group(fn):
            return jnp.concatenate([fn(g, q[g * rows_g:(g + 1) * rows_g]) for g in range(MOBA_KV_HEADS)], axis=0)

        past = lane < nblk
        block_scores = per_group(lambda g, qg: _dot_nt(qg, group(km, g)))
        sel = jnp.logical_and(_top_mask(jnp.where(past, block_scores, NEG_INF), min(MOBA_TOPK, nblk)), past)

        kn = kn_ref[...]
        vn = vn_ref[...]
        sn = per_group(lambda g, qg: _dot_nt(qg, group(kn, g))) * scale + bown_ref[...]
        tok = lax.broadcasted_iota(jnp.int32, sn.shape, 0) & (ds - 1)
        col = lax.broadcasted_iota(jnp.int32, sn.shape, 1)
        sn = jnp.where(col <= tok, sn, NEG_INF)
        m_own = jnp.max(sn, axis=-1, keepdims=True)
        p_own = jnp.exp(sn - m_own)
        l_own = jnp.sum(p_own, axis=-1, keepdims=True)
        pb = p_own.astype(BF16)
        o_own = jnp.concatenate([_dot(pb[g * rows_g:(g + 1) * rows_g], group(vn, g)) for g in range(MOBA_KV_HEADS)],
                                axis=0)

        m_top = jnp.maximum(jnp.max(jnp.where(sel, m_all, -jnp.inf), axis=-1, keepdims=True), m_own)
        w = jnp.where(sel, jnp.exp(m_all - m_top), 0.0)
        w_own = jnp.exp(m_own - m_top)
        den = jnp.sum(w * l_all, axis=-1, keepdims=True) + w_own * l_own
        num = w_own * o_own
        for b in range(nblk):
            num = num + w[:, b:b + 1] * part_ref[b]
        o_ref[...] = num / den


def _moba_sample_attn(page_table, q, bias, kn, vn, bown, cache_k, cache_v, ds):
    nseq, rows, _ = q.shape
    npages = page_table.shape[1]
    pg = min(SAMPLE_PAGES_PER_STEP, npages)
    nsteps = npages // pg
    nblk = npages * PAGE // MOBA_BLOCK
    kvw = MOBA_KV_HEADS * MOBA_DIM
    page_rows = PAGE * MOBA_KV_HEADS
    block_cols = MOBA_BLOCK * MOBA_KV_HEADS
    seq3 = lambda b, j, pt: (b, 0, 0)
    grid_spec = pltpu.PrefetchScalarGridSpec(
        num_scalar_prefetch=1,
        grid=(nseq, nsteps),
        in_specs=[pl.BlockSpec((None, rows, MOBA_DIM), seq3),
                  pl.BlockSpec((2, rows, block_cols), lambda b, j, pt: (0, 0, 0)),
                  pl.BlockSpec((None, NEW_PAD, kvw), seq3),
                  pl.BlockSpec((None, NEW_PAD, kvw), seq3),
                  pl.BlockSpec((rows, NEW_PAD), lambda b, j, pt: (0, 0))]
                 + [_page_spec(page_rows, MOBA_DIM, k, pg) for k in range(pg)]
                 + [_page_spec(page_rows, MOBA_DIM, k, pg) for k in range(pg)],
        out_specs=pl.BlockSpec((None, rows, MOBA_DIM), seq3),
        scratch_shapes=[pltpu.VMEM((LANES, kvw), F32), pltpu.VMEM((rows, LANES), F32), pltpu.VMEM((rows, LANES), F32),
                        pltpu.VMEM((nblk, rows, MOBA_DIM), F32)],
    )
    return pl.pallas_call(
        functools.partial(_moba_sample_kernel, ds=ds, pg=pg, nblk=nblk),
        grid_spec=grid_spec,
        out_shape=jax.ShapeDtypeStruct((nseq, rows, MOBA_DIM), F32),
        compiler_params=_cparams(("parallel", "arbitrary")),
        name="moba_sample_attn",
    )(page_table, q, bias, kn, vn, bown, *([cache_k] * pg), *([cache_v] * pg))


def _rope_tables(pos):
    inv = jnp.exp(-math.log(ROPE_THETA) * jnp.arange(0, MLA_ROPE, 2, dtype=F32) / MLA_ROPE)
    ang = pos.astype(F32)[:, None] * inv[None, :]
    return jnp.tile(jnp.cos(ang), (1, MLA_HEADS)), jnp.tile(jnp.sin(ang), (1, MLA_HEADS))


def _t5_bucket(dist):
    dist = jnp.maximum(dist, 0)
    max_exact = REL_BUCKETS // 2
    scaled = (jnp.log(jnp.maximum(dist, 1).astype(F32) / max_exact)
              / math.log(REL_MAX_DIST / max_exact) * (REL_BUCKETS - max_exact))
    large = jnp.minimum(max_exact + scaled.astype(jnp.int32), REL_BUCKETS - 1)
    return jnp.where(dist < max_exact, dist, large)


def _rope_select_matrices():
    half = MLA_ROPE // 2
    sela = np.zeros((MLA_HEADS * half, MLA_HEADS * LANES), np.float32)
    selb = np.zeros_like(sela)
    for h in range(MLA_HEADS):
        for r in range(half):
            sela[h * half + r, h * LANES + r] = 1.0
            selb[h * half + r, h * LANES + half + r] = 1.0
    return jnp.asarray(sela, BF16), jnp.asarray(selb, BF16)


def _prepare_weights(w_in, w_uq, w_uk, w_uv, w_a_out, w_b_out, w_o, peer_wq, peer_subkeys, peer_u, peer_v,
                     w_ple_gate, w_ple_proj):
    d = D_MODEL
    lo = MLA_LORA
    o_q, o_c, o_r = 0, lo, 2 * lo
    o_mq = o_r + MLA_ROPE
    o_mk = o_mq + MOBA_HEADS * MOBA_DIM
    o_mv = o_mk + MOBA_KV_HEADS * MOBA_DIM
    o_a = o_mv + MOBA_KV_HEADS * MOBA_DIM
    o_b = o_a + d
    cols = [w_in[:, o_a:o_b], w_in[:, o_b:o_b + d], w_in[:, o_q:o_c], w_in[:, o_c:o_r], w_in[:, o_mq:o_mk],
            w_in[:, o_mk:o_mv], w_in[:, o_mv:o_a], w_in[:, o_r:o_mq]]
    used = sum(c.shape[1] for c in cols)
    w_in_p = jnp.concatenate(cols + [jnp.zeros((d, Z_WIDTH - used), w_in.dtype)], axis=1).astype(BF16)

    hd = MLA_NOPE + MLA_ROPE
    half = MLA_ROPE // 2
    uq = w_uq.reshape(lo, MLA_HEADS, hd)
    w_uq_p = jnp.concatenate([uq[:, :, :MLA_NOPE].reshape(lo, -1), uq[:, :, MLA_NOPE:MLA_NOPE + half].reshape(lo, -1),
                              uq[:, :, MLA_NOPE + half:].reshape(lo, -1)], axis=1).astype(BF16)
    sela, selb = _rope_select_matrices()
    return dict(
        w_in=w_in_p, w_uq=w_uq_p, sela=sela, selb=selb,
        w_ukt=jnp.transpose(w_uk, (1, 2, 0)).astype(BF16),
        w_uv=jnp.transpose(w_uv, (1, 0, 2)).astype(BF16),
        w_a=w_a_out.astype(BF16), w_b=w_b_out.astype(BF16), w_o=w_o.astype(BF16),
        wqt=peer_wq.T.astype(BF16),
        sk=peer_subkeys.reshape(2 * PEER_HEADS, PEER_NKEYS, -1).astype(BF16),
        u=peer_u.astype(BF16), vt=peer_v.T.astype(BF16),
        w_gate=w_ple_gate.astype(BF16), w_proj=w_ple_proj.astype(BF16),
    )


def _row(v):
    return v.reshape(1, -1)


def _channel_tail(x1, h2, p, w, g_ple, g_final):
    u0, s1, tau, invz = _peer_route(h2, w["wqt"], w["sk"])
    ffn = _peer_dense(h2, w["u"], w["vt"], u0, s1, tau, invz)
    return _ple_final(x1, ffn, p, _row(g_ple), w["w_gate"], w["w_proj"], _row(g_final))


def _prompt_group(x, p, rel_bias, w, g_mix, g_q_lat, g_kv_lat, g_ffn, g_ple, g_final):
    batch, seq, d = x.shape
    t = batch * seq
    xf = x.reshape(t, d)
    z = _norm_matmul(xf, _row(g_mix), w["w_in"], Z_TILE)
    cos, sin = _rope_tables(jnp.tile(jnp.arange(seq), batch))
    ckv, krope, kc, kr, qa, qr = _mla_prep(z, cos, sin, _row(g_q_lat), w["w_uq"], _row(g_kv_lat), w["w_ukt"],
                                           w["sela"], w["selb"])
    oa = _mla_prompt_attn(qa, qr, kc, kr, w["w_uv"], batch, seq)

    nblk = seq // MOBA_BLOCK
    kmean, km, vm = _moba_kprep(z, seq)
    kmean = jnp.pad(kmean.reshape(batch, nblk, -1), ((0, 0), (0, LANES - nblk), (0, 0)))
    qm = _moba_select(z, kmean, seq)
    tb = _bias_table(rel_bias, _moba_tile_buckets())
    tb = jnp.transpose(tb.reshape(MOBA_HEADS, MOBA_TILE_KINDS, MOBA_BLOCK, MOBA_BLOCK), (1, 0, 2, 3))
    ob = _moba_prompt_attn(qm, km, vm, tb, batch, seq)

    x1, h2 = _merge(xf, oa, ob, z, w["w_a"], w["w_b"], w["w_o"], _row(g_ffn))
    y = _channel_tail(x1, h2, p.reshape(t, -1), w, g_ple, g_final)
    mk = z[:, OFF_MK:OFF_MV].reshape(batch, seq, MOBA_KV_HEADS, MOBA_DIM)
    mv = z[:, OFF_MV:OFF_R].reshape(batch, seq, MOBA_KV_HEADS, MOBA_DIM)
    return (y.reshape(batch, seq, d), ckv.reshape(batch, seq, -1), krope.reshape(batch, seq, -1), mk, mv)


def _head_major(a, nseq, ds):
    h, _, width = a.shape
    return jnp.transpose(a.reshape(h, nseq, ds, width), (1, 0, 2, 3)).reshape(nseq, h * ds, width)


def _pad_new(a, nseq, ds):
    return jnp.pad(a.reshape(nseq, ds, -1), ((0, 0), (0, NEW_PAD - ds), (0, 0)))


def _sample_group(x, p, page_table, caches, rel_bias, w, g_mix, g_q_lat, g_kv_lat, g_ffn, g_ple, g_final):
    cache_ckv, cache_krope, cache_k, cache_v = caches
    nseq, ds, d = x.shape
    t = nseq * ds
    npages = page_table.shape[1]
    past = npages * PAGE
    xf = x.reshape(t, d)
    z = _norm_matmul(xf, _row(g_mix), w["w_in"], Z_TILE)
    cos, sin = _rope_tables(jnp.tile(past + jnp.arange(ds), nseq))
    ckv, krope, kc, kr, qa, qr = _mla_prep(z, cos, sin, _row(g_q_lat), w["w_uq"], _row(g_kv_lat), w["w_ukt"],
                                           w["sela"], w["selb"])
    krope_t = jnp.transpose(cache_krope, (0, 1, 3, 2))
    o_lat = _mla_sample_attn(page_table, _head_major(qa, nseq, ds), _head_major(qr[:, :, :MLA_ROPE], nseq, ds),
                             _pad_new(kc, nseq, ds), _pad_new(kr[:, :MLA_ROPE], nseq, ds),
                             cache_ckv, krope_t, ds)
    o_lat = jnp.transpose(o_lat.reshape(nseq, MLA_HEADS, ds, -1), (1, 0, 2, 3)).reshape(MLA_HEADS, t, -1)
    oa = _head_proj(o_lat.astype(BF16), w["w_uv"])

    n_pool = cache_k.shape[1]
    ck = cache_k.reshape(1, n_pool, PAGE * MOBA_KV_HEADS, MOBA_DIM)
    cv = cache_v.reshape(1, n_pool, PAGE * MOBA_KV_HEADS, MOBA_DIM)
    mq = z[:, OFF_MQ:OFF_MK].astype(BF16).reshape(t, MOBA_HEADS, MOBA_DIM)
    mq = _head_major(jnp.transpose(mq, (1, 0, 2)), nseq, ds)

    col = jnp.arange(MOBA_BLOCK * MOBA_KV_HEADS)[None, :]
    dist_last = (MOBA_BLOCK + jnp.arange(ds)[:, None]) - col // MOBA_KV_HEADS
    rows_b = []
    for g in range(MOBA_KV_HEADS):
        mine = col % MOBA_KV_HEADS == g
        rows_b.append(jnp.where(mine, REL_BUCKETS - 1, BIAS_MASKED) + jnp.zeros((ds, 1), jnp.int32))
        rows_b.append(jnp.where(mine, _t5_bucket(dist_last), BIAS_MASKED))
    bias = _bias_table(rel_bias, jnp.concatenate(rows_b, axis=0))
    bias = bias.reshape(MOBA_HEADS, MOBA_KV_HEADS, 2, ds, -1)
    heads = np.arange(MOBA_HEADS)
    bias = jnp.transpose(bias[heads, heads // MOBA_GROUP], (1, 0, 2, 3)).reshape(2, MOBA_HEADS * ds, -1)
    own = _t5_bucket(jnp.arange(ds)[:, None] - jnp.arange(NEW_PAD)[None, :])
    bown = _bias_table(rel_bias, own).reshape(MOBA_HEADS * ds, NEW_PAD)
    kn = _pad_new(z[:, OFF_MK:OFF_MV].astype(BF16), nseq, ds)
    vn = _pad_new(z[:, OFF_MV:OFF_R].astype(BF16), nseq, ds)
    o_m = _moba_sample_attn(page_table, mq, bias, kn, vn, bown, ck, cv, ds)
    ob = jnp.transpose(o_m.reshape(nseq, MOBA_HEADS, ds, MOBA_DIM), (0, 2, 1, 3)).reshape(t, -1).astype(BF16)

    x1, h2 = _merge(xf, oa, ob, z, w["w_a"], w["w_b"], w["w_o"], _row(g_ffn))
    y = _channel_tail(x1, h2, p.reshape(t, -1), w, g_ple, g_final)
    mk = z[:, OFF_MK:OFF_MV].reshape(nseq, ds, MOBA_KV_HEADS, MOBA_DIM)
    mv = z[:, OFF_MV:OFF_R].reshape(nseq, ds, MOBA_KV_HEADS, MOBA_DIM)
    return (y.reshape(nseq, ds, d), ckv.reshape(nseq, ds, -1), krope.reshape(nseq, ds, -1), mk, mv)


def kernel(x_prompt, x_sample, cache_mla_ckv, cache_mla_krope, cache_moba_k, cache_moba_v, page_table,
           p_prompt, p_sample, rel_bias, g_mix, w_in, g_q_lat, w_uq, g_kv_lat, w_uk, w_uv, w_a_out,
           w_b_out, w_o, g_ffn, peer_wq, peer_subkeys, peer_u, peer_v, g_ple, w_ple_gate, w_ple_proj, g_final):
    assert g_mix.shape[0] == 1, "single-layer step"
    w = _prepare_weights(w_in[0], w_uq[0], w_uk[0], w_uv[0], w_a_out[0], w_b_out[0], w_o[0], peer_wq[0],
                         peer_subkeys[0], peer_u[0], peer_v[0], w_ple_gate[0], w_ple_proj[0])
    gains = (g_mix[0], g_q_lat[0], g_kv_lat[0], g_ffn[0], g_ple[0], g_final)
    yp, ckv_p, kr_p, k_p, v_p = _prompt_group(x_prompt, p_prompt[0], rel_bias, w, *gains)
    caches = (cache_mla_ckv, cache_mla_krope, cache_moba_k, cache_moba_v)
    ys, ckv_s, kr_s, k_s, v_s = _sample_group(x_sample, p_sample[0], page_table, caches, rel_bias, w, *gains)
    return (yp, ys, ckv_p[None], kr_p[None], k_p[None], v_p[None], ckv_s[None], kr_s[None], k_s[None], v_s[None])
```

```python
import functools
import math

import numpy as np
import jax
import jax.numpy as jnp
from jax import lax
from jax.experimental import pallas as pl
from jax.experimental.pallas import tpu as pltpu

F32 = jnp.float32
BF16 = jnp.bfloat16

D_MODEL = 2048
MLA_HEADS = 8
MLA_LORA = 512
MLA_NOPE = 128
MLA_ROPE = 64
MLA_V = 128
ROPE_THETA = 10000.0
MOBA_HEADS = 8
MOBA_KV_HEADS = 2
MOBA_GROUP = MOBA_HEADS // MOBA_KV_HEADS
MOBA_DIM = 128
MOBA_BLOCK = 256
MOBA_TOPK = 3
REL_BUCKETS = 32
REL_MAX_DIST = 128
PEER_HEADS = 8
PEER_NKEYS = 128
PEER_TOPK = 16
PLE_DIM = 256
PAGE = 128
NORM_EPS = 1e-6
NEG_INF = -1e30
MASK_BIG = -(2.0 ** 100)

LANES = 128
VMEM_LIMIT = 56 * 1024 * 1024

OFF_A = 0
OFF_B = D_MODEL
OFF_Q = 2 * D_MODEL
OFF_C = OFF_Q + MLA_LORA
OFF_MQ = OFF_C + MLA_LORA
OFF_MK = OFF_MQ + MOBA_HEADS * MOBA_DIM
OFF_MV = OFF_MK + MOBA_KV_HEADS * MOBA_DIM
OFF_R = OFF_MV + MOBA_KV_HEADS * MOBA_DIM
Z_WIDTH = 6912
Z_TILE = 1152


def _cparams(sem):
    return pltpu.CompilerParams(dimension_semantics=sem, vmem_limit_bytes=VMEM_LIMIT)


def _resident(shape):
    nd = len(shape)
    return pl.BlockSpec(shape, lambda *_: (0,) * nd, pipeline_mode=pl.Buffered(1))


def _dot(a, b):
    return jnp.dot(a, b, preferred_element_type=F32)


def _dot_nt(a, b):
    return lax.dot_general(a, b, (((1,), (1,)), ((), ())), preferred_element_type=F32)


def _rms(x, g):
    return x * lax.rsqrt(jnp.mean(x * x, axis=-1, keepdims=True) + NORM_EPS) * g


def _sigmoid(x):
    return 1.0 / (1.0 + jnp.exp(-x))


def _norm_matmul_kernel(x_ref, g_ref, w_ref, o_ref, h_ref):
    @pl.when(pl.program_id(1) == 0)
    def _():
        h_ref[...] = _rms(x_ref[...], g_ref[...]).astype(BF16)

    o_ref[...] = _dot(h_ref[...], w_ref[...])


def _norm_matmul(x, g, w, tn):
    m, k = x.shape
    n = w.shape[1]
    tm = min(512, m)
    return pl.pallas_call(
        _norm_matmul_kernel,
        grid=(m // tm, n // tn),
        in_specs=[pl.BlockSpec((tm, k), lambda i, j: (i, 0)),
                  pl.BlockSpec((1, k), lambda i, j: (0, 0)),
                  pl.BlockSpec((k, tn), lambda i, j: (0, j))],
        out_specs=pl.BlockSpec((tm, tn), lambda i, j: (i, j)),
        out_shape=jax.ShapeDtypeStruct((m, n), F32),
        scratch_shapes=[pltpu.VMEM((tm, k), BF16)],
        compiler_params=_cparams(("parallel", "arbitrary")),
        name="norm_matmul",
    )(x, g, w)


def _mla_prep_kernel(zq_ref, zc_ref, zr_ref, cos_ref, sin_ref, gq_ref, wuq_ref, gkv_ref, wukt_ref,
                     sela_ref, selb_ref, ckv_ref, krope_ref, kc_ref, kr_ref, qa_ref, qr_ref):
    nope = MLA_HEADS * MLA_NOPE
    half = MLA_HEADS * MLA_ROPE // 2
    q_all = _dot(_rms(zq_ref[...], gq_ref[...]).astype(BF16), wuq_ref[...])
    cos = cos_ref[...]
    sin = sin_ref[...]
    x1 = q_all[:, nope:nope + half]
    x2 = q_all[:, nope + half:]
    r1 = (x1 * cos - x2 * sin).astype(BF16)
    r2 = (x2 * cos + x1 * sin).astype(BF16)
    q_rope = _dot(r1, sela_ref[...]) + _dot(r2, selb_ref[...])
    for h in range(MLA_HEADS):
        qr_ref[h] = q_rope[:, h * LANES:(h + 1) * LANES].astype(BF16)
        q_nope = q_all[:, h * MLA_NOPE:(h + 1) * MLA_NOPE].astype(BF16)
        qa_ref[h] = _dot(q_nope, wukt_ref[h]).astype(BF16)

    ckv = _rms(zc_ref[...], gkv_ref[...])
    ckv_ref[...] = ckv
    kc_ref[...] = ckv.astype(BF16)

    zr = zr_ref[...]
    k1 = zr[:, :MLA_ROPE // 2]
    k2 = zr[:, MLA_ROPE // 2:MLA_ROPE]
    ck = cos[:, :MLA_ROPE // 2]
    sk = sin[:, :MLA_ROPE // 2]
    kr = jnp.concatenate([k1 * ck - k2 * sk, k2 * ck + k1 * sk], axis=1)
    krope_ref[...] = kr
    kr_ref[...] = jnp.concatenate([kr, jnp.zeros_like(kr)], axis=1).astype(BF16)


def _mla_prep(z, cos, sin, gq, wuq, gkv, wukt, sela, selb):
    t = z.shape[0]
    tm = min(256, t)
    row = lambda c: (lambda i: (i, c))
    hrow = lambda i: (0, i, 0)
    return pl.pallas_call(
        _mla_prep_kernel,
        grid=(t // tm,),
        in_specs=[pl.BlockSpec((tm, MLA_LORA), row(OFF_Q // MLA_LORA)),
                  pl.BlockSpec((tm, MLA_LORA), row(OFF_C // MLA_LORA)),
                  pl.BlockSpec((tm, LANES), row(OFF_R // LANES)),
                  pl.BlockSpec((tm, 256), row(0)),
                  pl.BlockSpec((tm, 256), row(0)),
                  _resident(gq.shape), _resident(wuq.shape), _resident(gkv.shape),
                  _resident(wukt.shape), _resident(sela.shape), _resident(selb.shape)],
        out_specs=[pl.BlockSpec((tm, MLA_LORA), row(0)),
                   pl.BlockSpec((tm, MLA_ROPE), row(0)),
                   pl.BlockSpec((tm, MLA_LORA), row(0)),
                   pl.BlockSpec((tm, LANES), row(0)),
                   pl.BlockSpec((MLA_HEADS, tm, MLA_LORA), hrow),
                   pl.BlockSpec((MLA_HEADS, tm, LANES), hrow)],
        out_shape=[jax.ShapeDtypeStruct((t, MLA_LORA), F32),
                   jax.ShapeDtypeStruct((t, MLA_ROPE), F32),
                   jax.ShapeDtypeStruct((t, MLA_LORA), BF16),
                   jax.ShapeDtypeStruct((t, LANES), BF16),
                   jax.ShapeDtypeStruct((MLA_HEADS, t, MLA_LORA), BF16),
                   jax.ShapeDtypeStruct((MLA_HEADS, t, LANES), BF16)],
        compiler_params=_cparams(("parallel",)),
        name="mla_prep",
    )(z, z, z, cos, sin, gq, wuq, gkv, wukt, sela, selb)


ATTN_SPLIT = 4


def _softmax_step(s, m_ref, l_ref, acc_ref, pv):
    m_prev = m_ref[...]
    m_new = jnp.maximum(m_prev, jnp.max(s, axis=-1, keepdims=True))
    alpha = jnp.exp(m_prev - m_new)
    p = jnp.exp(s - m_new)
    l_ref[...] = alpha * l_ref[...] + jnp.sum(p, axis=-1, keepdims=True)
    acc_ref[...] = alpha * acc_ref[...] + pv(p.astype(BF16))
    m_ref[...] = m_new


def _softmax_init(m_ref, l_ref, acc_ref):
    m_ref[...] = jnp.full(m_ref.shape, -jnp.inf, F32)
    l_ref[...] = jnp.zeros(l_ref.shape, F32)
    acc_ref[...] = jnp.zeros(acc_ref.shape, F32)


def _mla_attn_kernel(qa_ref, qr_ref, kc_ref, kr_ref, wuv_ref, o_ref, m_ref, l_ref, acc_ref, *, tq, tk):
    i = pl.program_id(1)
    j = pl.program_id(2)
    last = (i * tq + tq - 1) // tk
    rows = MLA_HEADS * tq
    scale = (MLA_NOPE + MLA_ROPE) ** -0.5

    @pl.when(j == 0)
    def _():
        _softmax_init(m_ref, l_ref, acc_ref)

    def pv(p):
        return _dot(p, kc_ref[...])

    def step(causal):
        hh = MLA_HEADS // ATTN_SPLIT
        nr = hh * tq
        ss = []
        for part in range(ATTN_SPLIT):
            qa = qa_ref[part * hh:(part + 1) * hh].reshape(nr, MLA_LORA)
            qr = qr_ref[part * hh:(part + 1) * hh].reshape(nr, LANES)
            ss.append((_dot_nt(qa, kc_ref[...]) + _dot_nt(qr, kr_ref[...])) * scale)
        for part in range(ATTN_SPLIT):
            s = ss[part]
            if causal:
                qpos = i * tq + (lax.broadcasted_iota(jnp.int32, s.shape, 0) & (tq - 1))
                kpos = j * tk + lax.broadcasted_iota(jnp.int32, s.shape, 1)
                s = jnp.where(kpos <= qpos, s, NEG_INF)
            sl = slice(part * nr, (part + 1) * nr)
            _softmax_step(s, m_ref.at[sl], l_ref.at[sl], acc_ref.at[sl], pv)

    @pl.when(j < last)
    def _():
        step(False)

    @pl.when(j == last)
    def _():
        step(True)
        o_lat = (acc_ref[...] / l_ref[...]).astype(BF16)
        for h in range(MLA_HEADS):
            o_ref[:, h * MLA_V:(h + 1) * MLA_V] = _dot(o_lat[h * tq:(h + 1) * tq], wuv_ref[h]).astype(BF16)


def _mla_prompt_attn(qa, qr, kc, kr, wuv, batch, seq):
    tq = min(256, seq)
    tk = min(512, seq)
    nq, nk = seq // tq, seq // tk
    t = batch * seq

    def qmap(b, i, j):
        return (0, b * nq + i, 0)

    def kmap(b, i, j):
        return (b * nk + jnp.minimum(j, (i * tq + tq - 1) // tk), 0)

    rows = MLA_HEADS * tq
    return pl.pallas_call(
        functools.partial(_mla_attn_kernel, tq=tq, tk=tk),
        grid=(batch, nq, nk),
        in_specs=[pl.BlockSpec((MLA_HEADS, tq, MLA_LORA), qmap),
                  pl.BlockSpec((MLA_HEADS, tq, LANES), qmap),
                  pl.BlockSpec((tk, MLA_LORA), kmap),
                  pl.BlockSpec((tk, LANES), kmap),
                  _resident(wuv.shape)],
        out_specs=pl.BlockSpec((tq, MLA_HEADS * MLA_V), lambda b, i, j: (b * nq + i, 0)),
        out_shape=jax.ShapeDtypeStruct((t, MLA_HEADS * MLA_V), BF16),
        scratch_shapes=[pltpu.VMEM((rows, 1), F32), pltpu.VMEM((rows, 1), F32),
                        pltpu.VMEM((rows, MLA_LORA), F32)],
        compiler_params=_cparams(("parallel", "parallel", "arbitrary")),
        name="mla_prompt_attn",
    )(qa, qr, kc, kr, wuv)


def _moba_kprep_kernel(zk_ref, zv_ref, mean_ref, km_ref, vm_ref, *, nblk):
    n = pl.program_id(0) % nblk
    k = zk_ref[...]
    v = zv_ref[...]
    mean_ref[0] = jnp.mean(k, axis=0, keepdims=True)
    onehot = jnp.where(lax.broadcasted_iota(jnp.int32, (MOBA_BLOCK, LANES), 1) == n, 1.0, 0.0).astype(BF16)
    for g in range(MOBA_KV_HEADS):
        km_ref[g] = jnp.concatenate([k[:, g * MOBA_DIM:(g + 1) * MOBA_DIM].astype(BF16), onehot], axis=1)
        vm_ref[g] = v[:, g * MOBA_DIM:(g + 1) * MOBA_DIM].astype(BF16)


def _moba_kprep(z, seq):
    t = z.shape[0]
    nb = t // MOBA_BLOCK
    kvw = MOBA_KV_HEADS * MOBA_DIM
    return pl.pallas_call(
        functools.partial(_moba_kprep_kernel, nblk=seq // MOBA_BLOCK),
        grid=(nb,),
        in_specs=[pl.BlockSpec((MOBA_BLOCK, kvw), lambda i: (i, OFF_MK // kvw)),
                  pl.BlockSpec((MOBA_BLOCK, kvw), lambda i: (i, OFF_MV // kvw))],
        out_specs=[pl.BlockSpec((1, 1, kvw), lambda i: (i, 0, 0)),
                   pl.BlockSpec((MOBA_KV_HEADS, MOBA_BLOCK, 2 * MOBA_DIM), lambda i: (0, i, 0)),
                   pl.BlockSpec((MOBA_KV_HEADS, MOBA_BLOCK, MOBA_DIM), lambda i: (0, i, 0))],
        out_shape=[jax.ShapeDtypeStruct((nb, 1, kvw), F32),
                   jax.ShapeDtypeStruct((MOBA_KV_HEADS, t, 2 * MOBA_DIM), BF16),
                   jax.ShapeDtypeStruct((MOBA_KV_HEADS, t, MOBA_DIM), BF16)],
        compiler_params=_cparams(("parallel",)),
        name="moba_kprep",
    )(z, z)


def _top_mask(s, k):
    lane = lax.broadcasted_iota(jnp.int32, s.shape, 1)
    sel = jnp.zeros(s.shape, jnp.bool_)
    for _ in range(k):
        m = jnp.max(s, axis=-1, keepdims=True)
        idx = jnp.min(jnp.where(s == m, lane, s.shape[1]), axis=-1, keepdims=True)
        hit = lane == idx
        sel = jnp.logical_or(sel, hit)
        s = jnp.where(hit, -jnp.inf, s)
    return sel


def _moba_select_kernel(zq_ref, km_ref, qm_ref, *, nblk):
    own = pl.program_id(0) % nblk
    zq = zq_ref[...]
    km = km_ref[...].astype(BF16)
    for h in range(MOBA_HEADS):
        g = h // MOBA_GROUP
        q = zq[:, h * MOBA_DIM:(h + 1) * MOBA_DIM].astype(BF16)
        s = _dot_nt(q, km[:, g * MOBA_DIM:(g + 1) * MOBA_DIM])
        lane = lax.broadcasted_iota(jnp.int32, s.shape, 1)
        past = lane < own
        sel = _top_mask(jnp.where(past, s, NEG_INF), MOBA_TOPK)
        mask = jnp.where(jnp.logical_and(past, jnp.logical_not(sel)), MASK_BIG, 0.0)
        qm_ref[h] = jnp.concatenate([q, mask.astype(BF16)], axis=1)


def _moba_select(z, kmean, seq):
    t = z.shape[0]
    nblk = seq // MOBA_BLOCK
    qw = MOBA_HEADS * MOBA_DIM
    return pl.pallas_call(
        functools.partial(_moba_select_kernel, nblk=nblk),
        grid=(t // MOBA_BLOCK,),
        in_specs=[pl.BlockSpec((MOBA_BLOCK, qw), lambda i: (i, OFF_MQ // qw)),
                  pl.BlockSpec((None, LANES, MOBA_KV_HEADS * MOBA_DIM), lambda i: (i // nblk, 0, 0))],
        out_specs=pl.BlockSpec((MOBA_HEADS, MOBA_BLOCK, 2 * MOBA_DIM), lambda i: (0, i, 0)),
        out_shape=jax.ShapeDtypeStruct((MOBA_HEADS, t, 2 * MOBA_DIM), BF16),
        compiler_params=_cparams(("parallel",)),
        name="moba_select",
    )(z, kmean)


BIAS_MASKED = REL_BUCKETS


def _bias_table_kernel(rb_ref, bucket_ref, o_ref):
    bucket = bucket_ref[...]
    for h in range(MOBA_HEADS):
        acc = jnp.where(bucket == BIAS_MASKED, NEG_INF, 0.0)
        for b in range(REL_BUCKETS):
            acc = jnp.where(bucket == b, rb_ref[b, h], acc)
        o_ref[h] = acc


def _bias_table(rel_bias, bucket):
    return pl.pallas_call(
        _bias_table_kernel,
        in_specs=[pl.BlockSpec(memory_space=pltpu.SMEM), pl.BlockSpec(memory_space=pltpu.VMEM)],
        out_specs=pl.BlockSpec(memory_space=pltpu.VMEM),
        out_shape=jax.ShapeDtypeStruct((MOBA_HEADS,) + bucket.shape, F32),
        name="bias_table",
    )(rel_bias, bucket)


MOBA_CHUNK_BLOCKS = 4
MOBA_TILE_KINDS = 4


def _moba_tile_buckets():
    r = jnp.arange(MOBA_BLOCK)[:, None]
    c = jnp.arange(MOBA_BLOCK)[None, :]
    far = jnp.full((MOBA_BLOCK, MOBA_BLOCK), REL_BUCKETS - 1, jnp.int32)
    prev = _t5_bucket(r - c + MOBA_BLOCK)
    own = jnp.where(c <= r, _t5_bucket(r - c), BIAS_MASKED)
    future = jnp.full((MOBA_BLOCK, MOBA_BLOCK), BIAS_MASKED, jnp.int32)
    return jnp.concatenate([far, prev, own, future], axis=0)


def _moba_attn_kernel(qm_ref, km_ref, vm_ref, tb_ref, o_ref, m_ref, l_ref, acc_ref, *, cb):
    i = pl.program_id(2)
    c = pl.program_id(3)
    rows = MOBA_GROUP * MOBA_BLOCK
    scale = MOBA_DIM ** -0.5

    @pl.when(c == 0)
    def _():
        _softmax_init(m_ref, l_ref, acc_ref)

    @pl.when(c * cb <= i)
    def _():
        hh = MOBA_GROUP // ATTN_SPLIT
        nr = hh * MOBA_BLOCK
        kinds = [jnp.clip(2 - (i - (c * cb + p)), 0, MOBA_TILE_KINDS - 1) for p in range(cb)]
        ss = [_dot_nt(qm_ref[part * hh:(part + 1) * hh].reshape(nr, 2 * MOBA_DIM), km_ref[0]) * scale
              for part in range(ATTN_SPLIT)]
        for part in range(ATTN_SPLIT):
            s = ss[part].reshape(hh, MOBA_BLOCK, cb * MOBA_BLOCK)
            s = jnp.concatenate([s[:, :, p * MOBA_BLOCK:(p + 1) * MOBA_BLOCK]
                                 + tb_ref[kinds[p], part * hh:(part + 1) * hh] for p in range(cb)], axis=2)
            sl = slice(part * nr, (part + 1) * nr)
            _softmax_step(s.reshape(nr, cb * MOBA_BLOCK), m_ref.at[sl], l_ref.at[sl], acc_ref.at[sl],
                          lambda pr: _dot(pr, vm_ref[0]))

    @pl.when(c == pl.num_programs(3) - 1)
    def _():
        o = acc_ref[...] / l_ref[...]
        for hh in range(MOBA_GROUP):
            o_ref[:, hh * MOBA_DIM:(hh + 1) * MOBA_DIM] = o[hh * MOBA_BLOCK:(hh + 1) * MOBA_BLOCK].astype(BF16)


def _moba_prompt_attn(qm, km, vm, tb, batch, seq):
    nblk = seq // MOBA_BLOCK
    cb = min(MOBA_CHUNK_BLOCKS, nblk)
    nchunk = nblk // cb
    t = batch * seq
    rows = MOBA_GROUP * MOBA_BLOCK

    def kmap(b, g, i, c):
        return (g, b * nchunk + jnp.minimum(c, i // cb), 0)

    return pl.pallas_call(
        functools.partial(_moba_attn_kernel, cb=cb),
        grid=(batch, MOBA_KV_HEADS, nblk, nchunk),
        in_specs=[pl.BlockSpec((MOBA_GROUP, MOBA_BLOCK, 2 * MOBA_DIM), lambda b, g, i, c: (g, b * nblk + i, 0)),
                  pl.BlockSpec((1, cb * MOBA_BLOCK, 2 * MOBA_DIM), kmap),
                  pl.BlockSpec((1, cb * MOBA_BLOCK, MOBA_DIM), kmap),
                  pl.BlockSpec((MOBA_TILE_KINDS, MOBA_GROUP, MOBA_BLOCK, MOBA_BLOCK), lambda b, g, i, c: (0, g, 0, 0))],
        out_specs=pl.BlockSpec((MOBA_BLOCK, MOBA_GROUP * MOBA_DIM), lambda b, g, i, c: (b * nblk + i, g)),
        out_shape=jax.ShapeDtypeStruct((t, MOBA_HEADS * MOBA_DIM), BF16),
        scratch_shapes=[pltpu.VMEM((rows, 1), F32), pltpu.VMEM((rows, 1), F32),
                        pltpu.VMEM((rows, MOBA_DIM), F32)],
        compiler_params=_cparams(("parallel", "parallel", "parallel", "arbitrary")),
        name="moba_prompt_attn",
    )(qm, km, vm, tb)


def _merge_kernel(x_ref, oa_ref, ob_ref, za_ref, zb_ref, wa_ref, wb_ref, wo_ref, g_ref, x1_ref, h_ref):
    a = _dot(oa_ref[...], wa_ref[...])
    b = _dot(ob_ref[...], wb_ref[...])
    y = _sigmoid(za_ref[...]) * a + _sigmoid(zb_ref[...]) * b
    x1 = x_ref[...] + _dot(y.astype(BF16), wo_ref[...])
    x1_ref[...] = x1
    h_ref[...] = _rms(x1, g_ref[...]).astype(BF16)


def _merge(x, oa, ob, z, wa, wb, wo, g):
    t, d = x.shape
    tm = min(256, t)
    row = lambda c: (lambda i: (i, c))
    return pl.pallas_call(
        _merge_kernel,
        grid=(t // tm,),
        in_specs=[pl.BlockSpec((tm, d), row(0)),
                  pl.BlockSpec((tm, oa.shape[1]), row(0)),
                  pl.BlockSpec((tm, ob.shape[1]), row(0)),
                  pl.BlockSpec((tm, d), row(OFF_A // d)),
                  pl.BlockSpec((tm, d), row(OFF_B // d)),
                  _resident(wa.shape), _resident(wb.shape), _resident(wo.shape), _resident(g.shape)],
        out_specs=[pl.BlockSpec((tm, d), row(0)), pl.BlockSpec((tm, d), row(0))],
        out_shape=[jax.ShapeDtypeStruct((t, d), F32), jax.ShapeDtypeStruct((t, d), BF16)],
        compiler_params=_cparams(("parallel",)),
        name="mixer_merge",
    )(x, oa, ob, z, z, wa, wb, wo, g)


def _top_rows(s, k):
    row = lax.broadcasted_iota(jnp.int32, s.shape, 0)
    rest = s
    vals = []
    for _ in range(k):
        m = jnp.max(rest, axis=0, keepdims=True)
        idx = jnp.min(jnp.where(rest == m, row, s.shape[0]), axis=0, keepdims=True)
        rest = jnp.where(row == idx, -jnp.inf, rest)
        vals.append(m)
    return jnp.concatenate(vals, axis=0), jnp.where(rest == -jnp.inf, s, -jnp.inf)


def _peer_route_kernel(h_ref, wqt_ref, sk_ref, th_ref, e0_ref, s1_ref, e1_ref):
    qt = _dot_nt(wqt_ref[...], h_ref[...])
    nk = PEER_NKEYS
    for p in range(PEER_HEADS):
        halves = []
        for half in range(2):
            g = 2 * p + half
            s = _dot(sk_ref[g], qt[g * nk:(g + 1) * nk].astype(BF16))
            v, kept = _top_rows(s, PEER_TOPK)
            halves.append((v - v[0:1], kept - v[0:1]))
        (v0, u0), (v1, u1) = halves
        hk = PEER_TOPK // 2
        cand = jnp.concatenate([v0[a:a + 1] + v1[:hk] for a in range(hk)]
                               + [v0[0:1] + v1[hk:], v0[hk:] + v1[0:1]], axis=0)
        best = _top_rows(cand, PEER_TOPK + 1)[0]
        tau = 0.5 * (best[PEER_TOPK - 1:PEER_TOPK] + best[PEER_TOPK:PEER_TOPK + 1])
        z = jnp.sum(jnp.where(cand >= tau, jnp.exp(cand), 0.0), axis=0, keepdims=True)
        th_ref[p] = tau - u0
        e0_ref[p] = jnp.exp(u0) * (1.0 / z)
        s1_ref[p] = u1
        e1_ref[p] = jnp.exp(u1)


def _peer_route(h, wqt, sk):
    t, d = h.shape
    tt = min(256, t)
    nk = PEER_NKEYS
    tok = lambda i: (0, 0, i)
    return pl.pallas_call(
        _peer_route_kernel,
        grid=(t // tt,),
        in_specs=[pl.BlockSpec((tt, d), lambda i: (i, 0)), _resident(wqt.shape), _resident(sk.shape)],
        out_specs=[pl.BlockSpec((PEER_HEADS, nk, tt), tok)] * 4,
        out_shape=[jax.ShapeDtypeStruct((PEER_HEADS, nk, t), F32)] * 4,
        compiler_params=_cparams(("parallel",)),
        name="peer_route",
    )(h, wqt, sk)


PEER_EXPERT_TILE = 1024
PEER_ROWS_PER_TILE = PEER_EXPERT_TILE // PEER_NKEYS
PEER_SUB_ROWS = 2


def _peer_dense_kernel(h_ref, u_ref, vt_ref, th_ref, e0_ref, s1_ref, e1_ref, o_ref, acc_ref, at0_ref, at1_ref):
    s = pl.program_id(1)
    nk = PEER_NKEYS
    sub = PEER_SUB_ROWS * nk
    nsub = PEER_EXPERT_TILE // sub

    @pl.when(s == 0)
    def _():
        acc_ref[...] = jnp.zeros(acc_ref.shape, F32)
        at1_ref[...] = jnp.zeros(at1_ref.shape, F32)

    def body(at_next_ref, at_ref):
        h = h_ref[...]
        at_next = [_dot_nt(u_ref[k * sub:(k + 1) * sub, :], h) for k in range(nsub)]
        out = None
        for k in range(nsub):
            at = at_ref[k * sub:(k + 1) * sub, :]
            act = 0.5 * at * (1.0 + lax.erf(at * (2.0 ** -0.5)))
            ws = []
            for r in range(PEER_SUB_ROWS):
                ii = k * PEER_SUB_ROWS + r
                gate = jnp.zeros((nk, at.shape[1]), F32)
                for p in range(PEER_HEADS):
                    picked = jnp.where(s1_ref[p] >= th_ref[p, ii:ii + 1, :], e1_ref[p], 0.0)
                    gate = gate + e0_ref[p, ii:ii + 1, :] * picked
                ws.append((gate * act[r * nk:(r + 1) * nk]).astype(BF16))
            part = _dot(vt_ref[:, k * sub:(k + 1) * sub], jnp.concatenate(ws, axis=0))
            out = part if out is None else out + part
        acc_ref[...] += out
        for k in range(nsub):
            at_next_ref[k * sub:(k + 1) * sub, :] = at_next[k]

    @pl.when(s % 2 == 0)
    def _():
        body(at0_ref, at1_ref)

    @pl.when(s % 2 == 1)
    def _():
        body(at1_ref, at0_ref)

    @pl.when(s == pl.num_programs(1) - 1)
    def _():
        o_ref[...] = acc_ref[...].T


def _peer_dense(h, u, vt, th, e0, s1, e1):
    t, d = h.shape
    tt = min(512, t)
    te = PEER_EXPERT_TILE
    ne = u.shape[0] // te
    nk = PEER_NKEYS
    nxt = lambda s: jnp.minimum(s, ne - 1)
    cur = lambda s: jnp.maximum(s - 1, 0)
    rows = pl.BlockSpec((PEER_HEADS, PEER_ROWS_PER_TILE, tt), lambda i, s: (0, cur(s), i))
    keys = pl.BlockSpec((PEER_HEADS, nk, tt), lambda i, s: (0, 0, i))
    return pl.pallas_call(
        _peer_dense_kernel,
        grid=(t // tt, ne + 1),
        in_specs=[pl.BlockSpec((tt, d), lambda i, s: (i, 0)),
                  pl.BlockSpec((te, d), lambda i, s: (nxt(s), 0)),
                  pl.BlockSpec((d, te), lambda i, s: (0, cur(s))),
                  rows, rows, keys, keys],
        out_specs=pl.BlockSpec((tt, d), lambda i, s: (i, 0)),
        out_shape=jax.ShapeDtypeStruct((t, d), F32),
        scratch_shapes=[pltpu.VMEM((d, tt), F32), pltpu.VMEM((te, tt), F32), pltpu.VMEM((te, tt), F32)],
        compiler_params=_cparams(("parallel", "arbitrary")),
        name="peer_dense",
    )(h, u, vt, th, e0, s1, e1)


def _ple_final_kernel(x_ref, f_ref, p_ref, gp_ref, wg_ref, wp_ref, gf_ref, o_ref):
    x = x_ref[...] + f_ref[...]
    gate = _sigmoid(_dot(_rms(x, gp_ref[...]).astype(BF16), wg_ref[...]))
    x = x + gate * _dot(p_ref[...].astype(BF16), wp_ref[...])
    o_ref[...] = _rms(x, gf_ref[...])


def _ple_final(x, ffn, p, gp, wg, wp, gf):
    t, d = x.shape
    tm = min(256, t)
    return pl.pallas_call(
        _ple_final_kernel,
        grid=(t // tm,),
        in_specs=[pl.BlockSpec((tm, d), lambda i: (i, 0)),
                  pl.BlockSpec((tm, d), lambda i: (i, 0)),
                  pl.BlockSpec((tm, p.shape[1]), lambda i: (i, 0)),
                  _resident(gp.shape), _resident(wg.shape), _resident(wp.shape), _resident(gf.shape)],
        out_specs=pl.BlockSpec((tm, d), lambda i: (i, 0)),
        out_shape=jax.ShapeDtypeStruct((t, d), F32),
        compiler_params=_cparams(("parallel",)),
        name="ple_final",
    )(x, ffn, p, gp, wg, wp, gf)


NEW_PAD = LANES
SAMPLE_PAGES_PER_STEP = 32


def _page_copies(pt_ref, b, j, slot, pg, streams):
    copies = []
    for k in range(pg):
        page = pt_ref[b, j * pg + k]
        for hbm, buf, sem in streams:
            copies.append(pltpu.make_async_copy(hbm.at[0, page], buf.at[slot, k], sem.at[slot]))
    return copies


def _paged_prefetch(pt_ref, pg, streams):
    b, j = pl.program_id(0), pl.program_id(1)
    nb, nj = pl.num_programs(0), pl.num_programs(1)
    g = b * nj + j
    slot = g % 2

    @pl.when(g == 0)
    def _():
        for c in _page_copies(pt_ref, b, j, slot, pg, streams):
            c.start()

    @pl.when(g + 1 < nb * nj)
    def _():
        wrap = j + 1 == nj
        for c in _page_copies(pt_ref, jnp.where(wrap, b + 1, b), jnp.where(wrap, 0, j + 1), 1 - slot, pg, streams):
            c.start()

    for c in _page_copies(pt_ref, b, j, slot, pg, streams):
        c.wait()
    return slot


def _mla_sample_kernel(pt_ref, qa_ref, qr_ref, kcn_ref, krn_ref, ckv_hbm, kr_hbm, o_ref, m_ref, l_ref, acc_ref,
                       cbuf, rbuf, csem, rsem, *, ds, pg):
    j = pl.program_id(1)
    scale = (MLA_NOPE + MLA_ROPE) ** -0.5
    slot = _paged_prefetch(pt_ref, pg, [(ckv_hbm, cbuf, csem), (kr_hbm, rbuf, rsem)])

    @pl.when(j == 0)
    def _():
        _softmax_init(m_ref, l_ref, acc_ref)

    qa = qa_ref[...]
    qr = qr_ref[...]
    cs = [cbuf[slot, k].astype(BF16) for k in range(pg)]
    s = jnp.concatenate([_dot_nt(qa, cs[k]) + _dot(qr, rbuf[slot, k].astype(BF16)) for k in range(pg)],
                        axis=1) * scale

    def pv(p):
        out = _dot(p[:, :PAGE], cs[0])
        for k in range(1, pg):
            out = out + _dot(p[:, k * PAGE:(k + 1) * PAGE], cs[k])
        return out

    _softmax_step(s, m_ref, l_ref, acc_ref, pv)

    @pl.when(j == pl.num_programs(1) - 1)
    def _():
        kcn = kcn_ref[...]
        sn = (_dot_nt(qa, kcn) + _dot_nt(qr, krn_ref[...])) * scale
        tok = lax.broadcasted_iota(jnp.int32, sn.shape, 0) & (ds - 1)
        col = lax.broadcasted_iota(jnp.int32, sn.shape, 1)
        sn = jnp.where(col <= tok, sn, NEG_INF)
        _softmax_step(sn, m_ref, l_ref, acc_ref, lambda p: _dot(p, kcn))
        o_ref[...] = acc_ref[...] / l_ref[...]


def _mla_sample_attn(page_table, qa, qr, kcn, krn, cache_ckv, cache_krope_t, ds):
    nseq, rows, _ = qa.shape
    pg = min(SAMPLE_PAGES_PER_STEP, page_table.shape[1])
    nsteps = page_table.shape[1] // pg
    seq3 = lambda b, j, pt: (b, 0, 0)
    grid_spec = pltpu.PrefetchScalarGridSpec(
        num_scalar_prefetch=1,
        grid=(nseq, nsteps),
        in_specs=[pl.BlockSpec((None, rows, MLA_LORA), seq3),
                  pl.BlockSpec((None, rows, MLA_ROPE), seq3),
                  pl.BlockSpec((None, NEW_PAD, MLA_LORA), seq3),
                  pl.BlockSpec((None, NEW_PAD, MLA_ROPE), seq3),
                  pl.BlockSpec(memory_space=pl.ANY),
                  pl.BlockSpec(memory_space=pl.ANY)],
        out_specs=pl.BlockSpec((None, rows, MLA_LORA), seq3),
        scratch_shapes=[pltpu.VMEM((rows, 1), F32), pltpu.VMEM((rows, 1), F32), pltpu.VMEM((rows, MLA_LORA), F32),
                        pltpu.VMEM((2, pg, PAGE, MLA_LORA), F32), pltpu.VMEM((2, pg, MLA_ROPE, PAGE), F32),
                        pltpu.SemaphoreType.DMA((2,)), pltpu.SemaphoreType.DMA((2,))],
    )
    return pl.pallas_call(
        functools.partial(_mla_sample_kernel, ds=ds, pg=pg),
        grid_spec=grid_spec,
        out_shape=jax.ShapeDtypeStruct((nseq, rows, MLA_LORA), F32),
        compiler_params=_cparams(("arbitrary", "arbitrary")),
        name="mla_sample_attn",
    )(page_table, qa, qr, kcn, krn, cache_ckv, cache_krope_t)


def _head_proj_kernel(o_ref, w_ref, out_ref):
    for h in range(MLA_HEADS):
        out_ref[:, h * MLA_V:(h + 1) * MLA_V] = _dot(o_ref[h], w_ref[h]).astype(BF16)


def _head_proj(o_lat, wuv):
    t = o_lat.shape[1]
    return pl.pallas_call(
        _head_proj_kernel,
        out_shape=jax.ShapeDtypeStruct((t, MLA_HEADS * MLA_V), BF16),
        name="mla_head_proj",
    )(o_lat, wuv)


def _moba_sample_kernel(pt_ref, q_ref, bias_ref, kn_ref, vn_ref, bown_ref, k_hbm, v_hbm, o_ref, mean_ref, m_ref,
                        l_ref, part_ref, kbuf, vbuf, ksem, vsem, *, ds, pg, nblk):
    slot = _paged_prefetch(pt_ref, pg, [(k_hbm, kbuf, ksem), (v_hbm, vbuf, vsem)])
    j = pl.program_id(1)
    last = pl.num_programs(1) - 1
    bps = pg // 2
    scale = MOBA_DIM ** -0.5
    rows = q_ref.shape[0]
    rows_g = rows // MOBA_KV_HEADS

    @pl.when(j == 0)
    def _():
        mean_ref[...] = jnp.zeros(mean_ref.shape, F32)
        m_ref[...] = jnp.zeros(m_ref.shape, F32)
        l_ref[...] = jnp.zeros(l_ref.shape, F32)

    q = q_ref[...]
    lane = lax.broadcasted_iota(jnp.int32, (rows, LANES), 1)
    sub_head = lax.broadcasted_iota(jnp.int32, (8, MOBA_DIM), 0) % MOBA_KV_HEADS
    m_all = m_ref[...]
    l_all = l_ref[...]
    means, scores = [], []
    for n in range(bps):
        ka = kbuf[slot, 2 * n]
        kb = kbuf[slot, 2 * n + 1]
        by_sublane = (ka.reshape(-1, 8, MOBA_DIM).sum(axis=0) + kb.reshape(-1, 8, MOBA_DIM).sum(axis=0))
        means.append(jnp.concatenate(
            [jnp.sum(jnp.where(sub_head == g, by_sublane, 0.0), axis=0, keepdims=True) for g in range(MOBA_KV_HEADS)],
            axis=1))
        kk = jnp.concatenate([ka, kb], axis=0).astype(BF16)
        tile = bias_ref[jnp.where(j == last, 1, 0)] if n == bps - 1 else bias_ref[0]
        scores.append(_dot_nt(q, kk) * scale + tile)
    probs = []
    for n in range(bps):
        mb = jnp.max(scores[n], axis=-1, keepdims=True)
        p = jnp.exp(scores[n] - mb)
        hit = lane == j * bps + n
        m_all = jnp.where(hit, mb, m_all)
        l_all = jnp.where(hit, jnp.sum(p, axis=-1, keepdims=True), l_all)
        probs.append(p.astype(BF16))
    for n in range(bps):
        vv = jnp.concatenate([vbuf[slot, 2 * n], vbuf[slot, 2 * n + 1]], axis=0).astype(BF16)
        part_ref[j * bps + n] = _dot(probs[n], vv)
    m_ref[...] = m_all
    l_ref[...] = l_all
    mean_ref[pl.ds(pl.multiple_of(j * bps, bps), bps), :] = jnp.concatenate(means, axis=0) * (1.0 / MOBA_BLOCK)

    @pl.when(j == last)
    def _():
        km = mean_ref[...].astype(BF16)

        def group(x, g):
            return x[:, g * MOBA_DIM:(g + 1) * MOBA_DIM]

        def per_group(fn):
            return jnp.concatenate([fn(g, q[g * rows_g:(g + 1) * rows_g]) for g in range(MOBA_KV_HEADS)], axis=0)

        past = lane < nblk
        block_scores = per_group(lambda g, qg: _dot_nt(qg, group(km, g)))
        sel = jnp.logical_and(_top_mask(jnp.where(past, block_scores, NEG_INF), min(MOBA_TOPK, nblk)), past)

        kn = kn_ref[...]
        vn = vn_ref[...]
        sn = per_group(lambda g, qg: _dot_nt(qg, group(kn, g))) * scale + bown_ref[...]
        tok = lax.broadcasted_iota(jnp.int32, sn.shape, 0) & (ds - 1)
        col = lax.broadcasted_iota(jnp.int32, sn.shape, 1)
        sn = jnp.where(col <= tok, sn, NEG_INF)
        m_own = jnp.max(sn, axis=-1, keepdims=True)
        p_own = jnp.exp(sn - m_own)
        l_own = jnp.sum(p_own, axis=-1, keepdims=True)
        pb = p_own.astype(BF16)
        o_own = jnp.concatenate([_dot(pb[g * rows_g:(g + 1) * rows_g], group(vn, g)) for g in range(MOBA_KV_HEADS)],
                                axis=0)

        m_top = jnp.maximum(jnp.max(jnp.where(sel, m_all, -jnp.inf), axis=-1, keepdims=True), m_own)
        w = jnp.where(sel, jnp.exp(m_all - m_top), 0.0)
        w_own = jnp.exp(m_own - m_top)
        den = jnp.sum(w * l_all, axis=-1, keepdims=True) + w_own * l_own
        num = w_own * o_own
        for b in range(nblk):
            num = num + w[:, b:b + 1] * part_ref[b]
        o_ref[...] = num / den


def _moba_sample_attn(page_table, q, bias, kn, vn, bown, cache_k, cache_v, ds):
    nseq, rows, _ = q.shape
    npages = page_table.shape[1]
    pg = min(SAMPLE_PAGES_PER_STEP, npages)
    nsteps = npages // pg
    nblk = npages * PAGE // MOBA_BLOCK
    kvw = MOBA_KV_HEADS * MOBA_DIM
    page_rows = PAGE * MOBA_KV_HEADS
    block_cols = MOBA_BLOCK * MOBA_KV_HEADS
    seq3 = lambda b, j, pt: (b, 0, 0)
    grid_spec = pltpu.PrefetchScalarGridSpec(
        num_scalar_prefetch=1,
        grid=(nseq, nsteps),
        in_specs=[pl.BlockSpec((None, rows, MOBA_DIM), seq3),
                  pl.BlockSpec((2, rows, block_cols), lambda b, j, pt: (0, 0, 0)),
                  pl.BlockSpec((None, NEW_PAD, kvw), seq3),
                  pl.BlockSpec((None, NEW_PAD, kvw), seq3),
                  pl.BlockSpec((rows, NEW_PAD), lambda b, j, pt: (0, 0)),
                  pl.BlockSpec(memory_space=pl.ANY),
                  pl.BlockSpec(memory_space=pl.ANY)],
        out_specs=pl.BlockSpec((None, rows, MOBA_DIM), seq3),
        scratch_shapes=[pltpu.VMEM((LANES, kvw), F32), pltpu.VMEM((rows, LANES), F32), pltpu.VMEM((rows, LANES), F32),
                        pltpu.VMEM((nblk, rows, MOBA_DIM), F32),
                        pltpu.VMEM((2, pg, page_rows, MOBA_DIM), F32), pltpu.VMEM((2, pg, page_rows, MOBA_DIM), F32),
                        pltpu.SemaphoreType.DMA((2,)), pltpu.SemaphoreType.DMA((2,))],
    )
    return pl.pallas_call(
        functools.partial(_moba_sample_kernel, ds=ds, pg=pg, nblk=nblk),
        grid_spec=grid_spec,
        out_shape=jax.ShapeDtypeStruct((nseq, rows, MOBA_DIM), F32),
        compiler_params=_cparams(("arbitrary", "arbitrary")),
        name="moba_sample_attn",
    )(page_table, q, bias, kn, vn, bown, cache_k, cache_v)


def _rope_tables(pos):
    inv = jnp.exp(-math.log(ROPE_THETA) * jnp.arange(0, MLA_ROPE, 2, dtype=F32) / MLA_ROPE)
    ang = pos.astype(F32)[:, None] * inv[None, :]
    return jnp.tile(jnp.cos(ang), (1, MLA_HEADS)), jnp.tile(jnp.sin(ang), (1, MLA_HEADS))


def _t5_bucket(dist):
    dist = jnp.maximum(dist, 0)
    max_exact = REL_BUCKETS // 2
    scaled = (jnp.log(jnp.maximum(dist, 1).astype(F32) / max_exact)
              / math.log(REL_MAX_DIST / max_exact) * (REL_BUCKETS - max_exact))
    large = jnp.minimum(max_exact + scaled.astype(jnp.int32), REL_BUCKETS - 1)
    return jnp.where(dist < max_exact, dist, large)


def _rope_select_matrices():
    half = MLA_ROPE // 2
    sela = np.zeros((MLA_HEADS * half, MLA_HEADS * LANES), np.float32)
    selb = np.zeros_like(sela)
    for h in range(MLA_HEADS):
        for r in range(half):
            sela[h * half + r, h * LANES + r] = 1.0
            selb[h * half + r, h * LANES + half + r] = 1.0
    return jnp.asarray(sela, BF16), jnp.asarray(selb, BF16)


def _prepare_weights(w_in, w_uq, w_uk, w_uv, w_a_out, w_b_out, w_o, peer_wq, peer_subkeys, peer_u, peer_v,
                     w_ple_gate, w_ple_proj):
    d = D_MODEL
    lo = MLA_LORA
    o_q, o_c, o_r = 0, lo, 2 * lo
    o_mq = o_r + MLA_ROPE
    o_mk = o_mq + MOBA_HEADS * MOBA_DIM
    o_mv = o_mk + MOBA_KV_HEADS * MOBA_DIM
    o_a = o_mv + MOBA_KV_HEADS * MOBA_DIM
    o_b = o_a + d
    cols = [w_in[:, o_a:o_b], w_in[:, o_b:o_b + d], w_in[:, o_q:o_c], w_in[:, o_c:o_r], w_in[:, o_mq:o_mk],
            w_in[:, o_mk:o_mv], w_in[:, o_mv:o_a], w_in[:, o_r:o_mq]]
    used = sum(c.shape[1] for c in cols)
    w_in_p = jnp.concatenate(cols + [jnp.zeros((d, Z_WIDTH - used), w_in.dtype)], axis=1).astype(BF16)

    hd = MLA_NOPE + MLA_ROPE
    half = MLA_ROPE // 2
    uq = w_uq.reshape(lo, MLA_HEADS, hd)
    w_uq_p = jnp.concatenate([uq[:, :, :MLA_NOPE].reshape(lo, -1), uq[:, :, MLA_NOPE:MLA_NOPE + half].reshape(lo, -1),
                              uq[:, :, MLA_NOPE + half:].reshape(lo, -1)], axis=1).astype(BF16)
    sela, selb = _rope_select_matrices()
    return dict(
        w_in=w_in_p, w_uq=w_uq_p, sela=sela, selb=selb,
        w_ukt=jnp.transpose(w_uk, (1, 2, 0)).astype(BF16),
        w_uv=jnp.transpose(w_uv, (1, 0, 2)).astype(BF16),
        w_a=w_a_out.astype(BF16), w_b=w_b_out.astype(BF16), w_o=w_o.astype(BF16),
        wqt=peer_wq.T.astype(BF16),
        sk=peer_subkeys.reshape(2 * PEER_HEADS, PEER_NKEYS, -1).astype(BF16),
        u=peer_u.astype(BF16), vt=peer_v.T.astype(BF16),
        w_gate=w_ple_gate.astype(BF16), w_proj=w_ple_proj.astype(BF16),
    )


def _row(v):
    return v.reshape(1, -1)


def _channel_tail(x1, h2, p, w, g_ple, g_final):
    th, e0, s1, e1 = _peer_route(h2, w["wqt"], w["sk"])
    ffn = _peer_dense(h2, w["u"], w["vt"], th, e0, s1, e1)
    return _ple_final(x1, ffn, p, _row(g_ple), w["w_gate"], w["w_proj"], _row(g_final))


def _prompt_group(x, p, rel_bias, w, g_mix, g_q_lat, g_kv_lat, g_ffn, g_ple, g_final):
    batch, seq, d = x.shape
    t = batch * seq
    xf = x.reshape(t, d)
    z = _norm_matmul(xf, _row(g_mix), w["w_in"], Z_TILE)
    cos, sin = _rope_tables(jnp.tile(jnp.arange(seq), batch))
    ckv, krope, kc, kr, qa, qr = _mla_prep(z, cos, sin, _row(g_q_lat), w["w_uq"], _row(g_kv_lat), w["w_ukt"],
                                           w["sela"], w["selb"])
    oa = _mla_prompt_attn(qa, qr, kc, kr, w["w_uv"], batch, seq)

    nblk = seq // MOBA_BLOCK
    kmean, km, vm = _moba_kprep(z, seq)
    kmean = jnp.pad(kmean.reshape(batch, nblk, -1), ((0, 0), (0, LANES - nblk), (0, 0)))
    qm = _moba_select(z, kmean, seq)
    tb = _bias_table(rel_bias, _moba_tile_buckets())
    tb = jnp.transpose(tb.reshape(MOBA_HEADS, MOBA_TILE_KINDS, MOBA_BLOCK, MOBA_BLOCK), (1, 0, 2, 3))
    ob = _moba_prompt_attn(qm, km, vm, tb, batch, seq)

    x1, h2 = _merge(xf, oa, ob, z, w["w_a"], w["w_b"], w["w_o"], _row(g_ffn))
    y = _channel_tail(x1, h2, p.reshape(t, -1), w, g_ple, g_final)
    mk = z[:, OFF_MK:OFF_MV].reshape(batch, seq, MOBA_KV_HEADS, MOBA_DIM)
    mv = z[:, OFF_MV:OFF_R].reshape(batch, seq, MOBA_KV_HEADS, MOBA_DIM)
    return (y.reshape(batch, seq, d), ckv.reshape(batch, seq, -1), krope.reshape(batch, seq, -1), mk, mv)


def _head_major(a, nseq, ds):
    h, _, width = a.shape
    return jnp.transpose(a.reshape(h, nseq, ds, width), (1, 0, 2, 3)).reshape(nseq, h * ds, width)


def _pad_new(a, nseq, ds):
    return jnp.pad(a.reshape(nseq, ds, -1), ((0, 0), (0, NEW_PAD - ds), (0, 0)))


def _sample_group(x, p, page_table, caches, rel_bias, w, g_mix, g_q_lat, g_kv_lat, g_ffn, g_ple, g_final):
    cache_ckv, cache_krope, cache_k, cache_v = caches
    nseq, ds, d = x.shape
    t = nseq * ds
    npages = page_table.shape[1]
    past = npages * PAGE
    xf = x.reshape(t, d)
    z = _norm_matmul(xf, _row(g_mix), w["w_in"], Z_TILE)
    cos, sin = _rope_tables(jnp.tile(past + jnp.arange(ds), nseq))
    ckv, krope, kc, kr, qa, qr = _mla_prep(z, cos, sin, _row(g_q_lat), w["w_uq"], _row(g_kv_lat), w["w_ukt"],
                                           w["sela"], w["selb"])
    krope_t = jnp.transpose(cache_krope, (0, 1, 3, 2))
    o_lat = _mla_sample_attn(page_table, _head_major(qa, nseq, ds), _head_major(qr[:, :, :MLA_ROPE], nseq, ds),
                             _pad_new(kc, nseq, ds), _pad_new(kr[:, :MLA_ROPE], nseq, ds),
                             cache_ckv, krope_t, ds)
    o_lat = jnp.transpose(o_lat.reshape(nseq, MLA_HEADS, ds, -1), (1, 0, 2, 3)).reshape(MLA_HEADS, t, -1)
    oa = _head_proj(o_lat.astype(BF16), w["w_uv"])

    n_pool = cache_k.shape[1]
    ck = cache_k.reshape(1, n_pool, PAGE * MOBA_KV_HEADS, MOBA_DIM)
    cv = cache_v.reshape(1, n_pool, PAGE * MOBA_KV_HEADS, MOBA_DIM)
    mq = z[:, OFF_MQ:OFF_MK].astype(BF16).reshape(t, MOBA_HEADS, MOBA_DIM)
    mq = _head_major(jnp.transpose(mq, (1, 0, 2)), nseq, ds)

    col = jnp.arange(MOBA_BLOCK * MOBA_KV_HEADS)[None, :]
    dist_last = (MOBA_BLOCK + jnp.arange(ds)[:, None]) - col // MOBA_KV_HEADS
    rows_b = []
    for g in range(MOBA_KV_HEADS):
        mine = col % MOBA_KV_HEADS == g
        rows_b.append(jnp.where(mine, REL_BUCKETS - 1, BIAS_MASKED) + jnp.zeros((ds, 1), jnp.int32))
        rows_b.append(jnp.where(mine, _t5_bucket(dist_last), BIAS_MASKED))
    bias = _bias_table(rel_bias, jnp.concatenate(rows_b, axis=0))
    bias = bias.reshape(MOBA_HEADS, MOBA_KV_HEADS, 2, ds, -1)
    heads = np.arange(MOBA_HEADS)
    bias = jnp.transpose(bias[heads, heads // MOBA_GROUP], (1, 0, 2, 3)).reshape(2, MOBA_HEADS * ds, -1)
    own = _t5_bucket(jnp.arange(ds)[:, None] - jnp.arange(NEW_PAD)[None, :])
    bown = _bias_table(rel_bias, own).reshape(MOBA_HEADS * ds, NEW_PAD)
    kn = _pad_new(z[:, OFF_MK:OFF_MV].astype(BF16), nseq, ds)
    vn = _pad_new(z[:, OFF_MV:OFF_R].astype(BF16), nseq, ds)
    o_m = _moba_sample_attn(page_table, mq, bias, kn, vn, bown, ck, cv, ds)
    ob = jnp.transpose(o_m.reshape(nseq, MOBA_HEADS, ds, MOBA_DIM), (0, 2, 1, 3)).reshape(t, -1).astype(BF16)

    x1, h2 = _merge(xf, oa, ob, z, w["w_a"], w["w_b"], w["w_o"], _row(g_ffn))
    y = _channel_tail(x1, h2, p.reshape(t, -1), w, g_ple, g_final)
    mk = z[:, OFF_MK:OFF_MV].reshape(nseq, ds, MOBA_KV_HEADS, MOBA_DIM)
    mv = z[:, OFF_MV:OFF_R].reshape(nseq, ds, MOBA_KV_HEADS, MOBA_DIM)
    return (y.reshape(nseq, ds, d), ckv.reshape(nseq, ds, -1), krope.reshape(nseq, ds, -1), mk, mv)


def kernel(x_prompt, x_sample, cache_mla_ckv, cache_mla_krope, cache_moba_k, cache_moba_v, page_table,
           p_prompt, p_sample, rel_bias, g_mix, w_in, g_q_lat, w_uq, g_kv_lat, w_uk, w_uv, w_a_out,
           w_b_out, w_o, g_ffn, peer_wq, peer_subkeys, peer_u, peer_v, g_ple, w_ple_gate, w_ple_proj, g_final):
    assert g_mix.shape[0] == 1, "single-layer step"
    w = _prepare_weights(w_in[0], w_uq[0], w_uk[0], w_uv[0], w_a_out[0], w_b_out[0], w_o[0], peer_wq[0],
                         peer_subkeys[0], peer_u[0], peer_v[0], w_ple_gate[0], w_ple_proj[0])
    gains = (g_mix[0], g_q_lat[0], g_kv_lat[0], g_ffn[0], g_ple[0], g_final)
    yp, ckv_p, kr_p, k_p, v_p = _prompt_group(x_prompt, p_prompt[0], rel_bias, w, *gains)
    caches = (cache_mla_ckv, cache_mla_krope, cache_moba_k, cache_moba_v)
    ys, ckv_s, kr_s, k_s, v_s = _sample_group(x_sample, p_sample[0], page_table, caches, rel_bias, w, *gains)
    return (yp, ys, ckv_p[None], kr_p[None], k_p[None], v_p[None], ckv_s[None], kr_s[None], k_s[None], v_s[None])
```

```python
import functools
import math

import numpy as np
import jax
import jax.numpy as jnp
from jax import lax
from jax.experimental import pallas as pl
from jax.experimental.pallas import tpu as pltpu

F32 = jnp.float32
BF16 = jnp.bfloat16

D_MODEL = 2048
MLA_HEADS = 8
MLA_LORA = 512
MLA_NOPE = 128
MLA_ROPE = 64
MLA_V = 128
ROPE_THETA = 10000.0
MOBA_HEADS = 8
MOBA_KV_HEADS = 2
MOBA_GROUP = MOBA_HEADS // MOBA_KV_HEADS
MOBA_DIM = 128
MOBA_BLOCK = 256
MOBA_TOPK = 3
REL_BUCKETS = 32
REL_MAX_DIST = 128
PEER_HEADS = 8
PEER_NKEYS = 128
PEER_TOPK = 16
PLE_DIM = 256
PAGE = 128
NORM_EPS = 1e-6
NEG_INF = -1e30
MASK_BIG = -(2.0 ** 100)

LANES = 128
VMEM_LIMIT = 56 * 1024 * 1024

OFF_A = 0
OFF_B = D_MODEL
OFF_Q = 2 * D_MODEL
OFF_C = OFF_Q + MLA_LORA
OFF_MQ = OFF_C + MLA_LORA
OFF_MK = OFF_MQ + MOBA_HEADS * MOBA_DIM
OFF_MV = OFF_MK + MOBA_KV_HEADS * MOBA_DIM
OFF_R = OFF_MV + MOBA_KV_HEADS * MOBA_DIM
Z_WIDTH = 6912
Z_TILE = 1152


def _cparams(sem):
    return pltpu.CompilerParams(dimension_semantics=sem, vmem_limit_bytes=VMEM_LIMIT)


def _resident(shape):
    nd = len(shape)
    return pl.BlockSpec(shape, lambda *_: (0,) * nd, pipeline_mode=pl.Buffered(1))


def _dot(a, b):
    return jnp.dot(a, b, preferred_element_type=F32)


def _dot_nt(a, b):
    return lax.dot_general(a, b, (((1,), (1,)), ((), ())), preferred_element_type=F32)


def _rms(x, g):
    return x * lax.rsqrt(jnp.mean(x * x, axis=-1, keepdims=True) + NORM_EPS) * g


def _sigmoid(x):
    return 1.0 / (1.0 + jnp.exp(-x))


def _norm_matmul_kernel(x_ref, g_ref, w_ref, o_ref, h_ref):
    @pl.when(pl.program_id(1) == 0)
    def _():
        h_ref[...] = _rms(x_ref[...], g_ref[...]).astype(BF16)

    o_ref[...] = _dot(h_ref[...], w_ref[...])


def _norm_matmul(x, g, w, tn):
    m, k = x.shape
    n = w.shape[1]
    tm = min(512, m)
    return pl.pallas_call(
        _norm_matmul_kernel,
        grid=(m // tm, n // tn),
        in_specs=[pl.BlockSpec((tm, k), lambda i, j: (i, 0)),
                  pl.BlockSpec((1, k), lambda i, j: (0, 0)),
                  pl.BlockSpec((k, tn), lambda i, j: (0, j))],
        out_specs=pl.BlockSpec((tm, tn), lambda i, j: (i, j)),
        out_shape=jax.ShapeDtypeStruct((m, n), F32),
        scratch_shapes=[pltpu.VMEM((tm, k), BF16)],
        compiler_params=_cparams(("parallel", "arbitrary")),
        name="norm_matmul",
    )(x, g, w)


def _mla_prep_kernel(zq_ref, zc_ref, zr_ref, cos_ref, sin_ref, gq_ref, wuq_ref, gkv_ref, wukt_ref,
                     sela_ref, selb_ref, ckv_ref, krope_ref, kc_ref, kr_ref, qa_ref, qr_ref):
    nope = MLA_HEADS * MLA_NOPE
    half = MLA_HEADS * MLA_ROPE // 2
    q_all = _dot(_rms(zq_ref[...], gq_ref[...]).astype(BF16), wuq_ref[...])
    cos = cos_ref[...]
    sin = sin_ref[...]
    x1 = q_all[:, nope:nope + half]
    x2 = q_all[:, nope + half:]
    r1 = (x1 * cos - x2 * sin).astype(BF16)
    r2 = (x2 * cos + x1 * sin).astype(BF16)
    q_rope = _dot(r1, sela_ref[...]) + _dot(r2, selb_ref[...])
    for h in range(MLA_HEADS):
        qr_ref[h] = q_rope[:, h * LANES:(h + 1) * LANES].astype(BF16)
        q_nope = q_all[:, h * MLA_NOPE:(h + 1) * MLA_NOPE].astype(BF16)
        qa_ref[h] = _dot(q_nope, wukt_ref[h]).astype(BF16)

    ckv = _rms(zc_ref[...], gkv_ref[...])
    ckv_ref[...] = ckv
    kc_ref[...] = ckv.astype(BF16)

    zr = zr_ref[...]
    k1 = zr[:, :MLA_ROPE // 2]
    k2 = zr[:, MLA_ROPE // 2:MLA_ROPE]
    ck = cos[:, :MLA_ROPE // 2]
    sk = sin[:, :MLA_ROPE // 2]
    kr = jnp.concatenate([k1 * ck - k2 * sk, k2 * ck + k1 * sk], axis=1)
    krope_ref[...] = kr
    kr_ref[...] = jnp.concatenate([kr, jnp.zeros_like(kr)], axis=1).astype(BF16)


def _mla_prep(z, cos, sin, gq, wuq, gkv, wukt, sela, selb):
    t = z.shape[0]
    tm = min(256, t)
    row = lambda c: (lambda i: (i, c))
    hrow = lambda i: (0, i, 0)
    return pl.pallas_call(
        _mla_prep_kernel,
        grid=(t // tm,),
        in_specs=[pl.BlockSpec((tm, MLA_LORA), row(OFF_Q // MLA_LORA)),
                  pl.BlockSpec((tm, MLA_LORA), row(OFF_C // MLA_LORA)),
                  pl.BlockSpec((tm, LANES), row(OFF_R // LANES)),
                  pl.BlockSpec((tm, 256), row(0)),
                  pl.BlockSpec((tm, 256), row(0)),
                  _resident(gq.shape), _resident(wuq.shape), _resident(gkv.shape),
                  _resident(wukt.shape), _resident(sela.shape), _resident(selb.shape)],
        out_specs=[pl.BlockSpec((tm, MLA_LORA), row(0)),
                   pl.BlockSpec((tm, MLA_ROPE), row(0)),
                   pl.BlockSpec((tm, MLA_LORA), row(0)),
                   pl.BlockSpec((tm, LANES), row(0)),
                   pl.BlockSpec((MLA_HEADS, tm, MLA_LORA), hrow),
                   pl.BlockSpec((MLA_HEADS, tm, LANES), hrow)],
        out_shape=[jax.ShapeDtypeStruct((t, MLA_LORA), F32),
                   jax.ShapeDtypeStruct((t, MLA_ROPE), F32),
                   jax.ShapeDtypeStruct((t, MLA_LORA), BF16),
                   jax.ShapeDtypeStruct((t, LANES), BF16),
                   jax.ShapeDtypeStruct((MLA_HEADS, t, MLA_LORA), BF16),
                   jax.ShapeDtypeStruct((MLA_HEADS, t, LANES), BF16)],
        compiler_params=_cparams(("parallel",)),
        name="mla_prep",
    )(z, z, z, cos, sin, gq, wuq, gkv, wukt, sela, selb)


ATTN_SPLIT = 4


def _softmax_step(s, m_ref, l_ref, acc_ref, pv):
    m_prev = m_ref[...]
    m_new = jnp.maximum(m_prev, jnp.max(s, axis=-1, keepdims=True))
    alpha = jnp.exp(m_prev - m_new)
    p = jnp.exp(s - m_new)
    l_ref[...] = alpha * l_ref[...] + jnp.sum(p, axis=-1, keepdims=True)
    acc_ref[...] = alpha * acc_ref[...] + pv(p.astype(BF16))
    m_ref[...] = m_new


def _softmax_init(m_ref, l_ref, acc_ref):
    m_ref[...] = jnp.full(m_ref.shape, -jnp.inf, F32)
    l_ref[...] = jnp.zeros(l_ref.shape, F32)
    acc_ref[...] = jnp.zeros(acc_ref.shape, F32)


def _mla_attn_kernel(qa_ref, qr_ref, kc_ref, kr_ref, wuv_ref, o_ref, m_ref, l_ref, acc_ref, *, tq, tk):
    i = pl.program_id(1)
    j = pl.program_id(2)
    last = (i * tq + tq - 1) // tk
    rows = MLA_HEADS * tq
    scale = (MLA_NOPE + MLA_ROPE) ** -0.5

    @pl.when(j == 0)
    def _():
        _softmax_init(m_ref, l_ref, acc_ref)

    def pv(p):
        return _dot(p, kc_ref[...])

    def step(causal):
        hh = MLA_HEADS // ATTN_SPLIT
        nr = hh * tq
        ss = []
        for part in range(ATTN_SPLIT):
            qa = qa_ref[part * hh:(part + 1) * hh].reshape(nr, MLA_LORA)
            qr = qr_ref[part * hh:(part + 1) * hh].reshape(nr, LANES)
            ss.append((_dot_nt(qa, kc_ref[...]) + _dot_nt(qr, kr_ref[...])) * scale)
        for part in range(ATTN_SPLIT):
            s = ss[part]
            if causal:
                qpos = i * tq + (lax.broadcasted_iota(jnp.int32, s.shape, 0) & (tq - 1))
                kpos = j * tk + lax.broadcasted_iota(jnp.int32, s.shape, 1)
                s = jnp.where(kpos <= qpos, s, NEG_INF)
            sl = slice(part * nr, (part + 1) * nr)
            _softmax_step(s, m_ref.at[sl], l_ref.at[sl], acc_ref.at[sl], pv)

    @pl.when(j < last)
    def _():
        step(False)

    @pl.when(j == last)
    def _():
        step(True)
        o_lat = (acc_ref[...] / l_ref[...]).astype(BF16)
        for h in range(MLA_HEADS):
            o_ref[:, h * MLA_V:(h + 1) * MLA_V] = _dot(o_lat[h * tq:(h + 1) * tq], wuv_ref[h]).astype(BF16)


def _mla_prompt_attn(qa, qr, kc, kr, wuv, batch, seq):
    tq = min(256, seq)
    tk = min(512, seq)
    nq, nk = seq // tq, seq // tk
    t = batch * seq

    def qmap(b, i, j):
        return (0, b * nq + i, 0)

    def kmap(b, i, j):
        return (b * nk + jnp.minimum(j, (i * tq + tq - 1) // tk), 0)

    rows = MLA_HEADS * tq
    return pl.pallas_call(
        functools.partial(_mla_attn_kernel, tq=tq, tk=tk),
        grid=(batch, nq, nk),
        in_specs=[pl.BlockSpec((MLA_HEADS, tq, MLA_LORA), qmap),
                  pl.BlockSpec((MLA_HEADS, tq, LANES), qmap),
                  pl.BlockSpec((tk, MLA_LORA), kmap),
                  pl.BlockSpec((tk, LANES), kmap),
                  _resident(wuv.shape)],
        out_specs=pl.BlockSpec((tq, MLA_HEADS * MLA_V), lambda b, i, j: (b * nq + i, 0)),
        out_shape=jax.ShapeDtypeStruct((t, MLA_HEADS * MLA_V), BF16),
        scratch_shapes=[pltpu.VMEM((rows, 1), F32), pltpu.VMEM((rows, 1), F32),
                        pltpu.VMEM((rows, MLA_LORA), F32)],
        compiler_params=_cparams(("parallel", "parallel", "arbitrary")),
        name="mla_prompt_attn",
    )(qa, qr, kc, kr, wuv)


def _moba_kprep_kernel(zk_ref, zv_ref, mean_ref, km_ref, vm_ref, *, nblk):
    n = pl.program_id(0) % nblk
    k = zk_ref[...]
    v = zv_ref[...]
    mean_ref[0] = jnp.mean(k, axis=0, keepdims=True)
    onehot = jnp.where(lax.broadcasted_iota(jnp.int32, (MOBA_BLOCK, LANES), 1) == n, 1.0, 0.0).astype(BF16)
    for g in range(MOBA_KV_HEADS):
        km_ref[g] = jnp.concatenate([k[:, g * MOBA_DIM:(g + 1) * MOBA_DIM].astype(BF16), onehot], axis=1)
        vm_ref[g] = v[:, g * MOBA_DIM:(g + 1) * MOBA_DIM].astype(BF16)


def _moba_kprep(z, seq):
    t = z.shape[0]
    nb = t // MOBA_BLOCK
    kvw = MOBA_KV_HEADS * MOBA_DIM
    return pl.pallas_call(
        functools.partial(_moba_kprep_kernel, nblk=seq // MOBA_BLOCK),
        grid=(nb,),
        in_specs=[pl.BlockSpec((MOBA_BLOCK, kvw), lambda i: (i, OFF_MK // kvw)),
                  pl.BlockSpec((MOBA_BLOCK, kvw), lambda i: (i, OFF_MV // kvw))],
        out_specs=[pl.BlockSpec((1, 1, kvw), lambda i: (i, 0, 0)),
                   pl.BlockSpec((MOBA_KV_HEADS, MOBA_BLOCK, 2 * MOBA_DIM), lambda i: (0, i, 0)),
                   pl.BlockSpec((MOBA_KV_HEADS, MOBA_BLOCK, MOBA_DIM), lambda i: (0, i, 0))],
        out_shape=[jax.ShapeDtypeStruct((nb, 1, kvw), F32),
                   jax.ShapeDtypeStruct((MOBA_KV_HEADS, t, 2 * MOBA_DIM), BF16),
                   jax.ShapeDtypeStruct((MOBA_KV_HEADS, t, MOBA_DIM), BF16)],
        compiler_params=_cparams(("parallel",)),
        name="moba_kprep",
    )(z, z)


def _top_mask(s, k):
    lane = lax.broadcasted_iota(jnp.int32, s.shape, 1)
    sel = jnp.zeros(s.shape, jnp.bool_)
    for _ in range(k):
        m = jnp.max(s, axis=-1, keepdims=True)
        idx = jnp.min(jnp.where(s == m, lane, s.shape[1]), axis=-1, keepdims=True)
        hit = lane == idx
        sel = jnp.logical_or(sel, hit)
        s = jnp.where(hit, -jnp.inf, s)
    return sel


def _moba_select_kernel(zq_ref, km_ref, qm_ref, *, nblk):
    own = pl.program_id(0) % nblk
    zq = zq_ref[...]
    km = km_ref[...].astype(BF16)
    for h in range(MOBA_HEADS):
        g = h // MOBA_GROUP
        q = zq[:, h * MOBA_DIM:(h + 1) * MOBA_DIM].astype(BF16)
        s = _dot_nt(q, km[:, g * MOBA_DIM:(g + 1) * MOBA_DIM])
        lane = lax.broadcasted_iota(jnp.int32, s.shape, 1)
        past = lane < own
        sel = _top_mask(jnp.where(past, s, NEG_INF), MOBA_TOPK)
        mask = jnp.where(jnp.logical_and(past, jnp.logical_not(sel)), MASK_BIG, 0.0)
        qm_ref[h] = jnp.concatenate([q, mask.astype(BF16)], axis=1)


def _moba_select(z, kmean, seq):
    t = z.shape[0]
    nblk = seq // MOBA_BLOCK
    qw = MOBA_HEADS * MOBA_DIM
    return pl.pallas_call(
        functools.partial(_moba_select_kernel, nblk=nblk),
        grid=(t // MOBA_BLOCK,),
        in_specs=[pl.BlockSpec((MOBA_BLOCK, qw), lambda i: (i, OFF_MQ // qw)),
                  pl.BlockSpec((None, LANES, MOBA_KV_HEADS * MOBA_DIM), lambda i: (i // nblk, 0, 0))],
        out_specs=pl.BlockSpec((MOBA_HEADS, MOBA_BLOCK, 2 * MOBA_DIM), lambda i: (0, i, 0)),
        out_shape=jax.ShapeDtypeStruct((MOBA_HEADS, t, 2 * MOBA_DIM), BF16),
        compiler_params=_cparams(("parallel",)),
        name="moba_select",
    )(z, kmean)


BIAS_MASKED = REL_BUCKETS


def _bias_table_kernel(rb_ref, bucket_ref, o_ref):
    bucket = bucket_ref[...]
    for h in range(MOBA_HEADS):
        acc = jnp.where(bucket == BIAS_MASKED, NEG_INF, 0.0)
        for b in range(REL_BUCKETS):
            acc = jnp.where(bucket == b, rb_ref[b, h], acc)
        o_ref[h] = acc


def _bias_table(rel_bias, bucket):
    return pl.pallas_call(
        _bias_table_kernel,
        in_specs=[pl.BlockSpec(memory_space=pltpu.SMEM), pl.BlockSpec(memory_space=pltpu.VMEM)],
        out_specs=pl.BlockSpec(memory_space=pltpu.VMEM),
        out_shape=jax.ShapeDtypeStruct((MOBA_HEADS,) + bucket.shape, F32),
        name="bias_table",
    )(rel_bias, bucket)


MOBA_CHUNK_BLOCKS = 4
MOBA_TILE_KINDS = 4


def _moba_tile_buckets():
    r = jnp.arange(MOBA_BLOCK)[:, None]
    c = jnp.arange(MOBA_BLOCK)[None, :]
    far = jnp.full((MOBA_BLOCK, MOBA_BLOCK), REL_BUCKETS - 1, jnp.int32)
    prev = _t5_bucket(r - c + MOBA_BLOCK)
    own = jnp.where(c <= r, _t5_bucket(r - c), BIAS_MASKED)
    future = jnp.full((MOBA_BLOCK, MOBA_BLOCK), BIAS_MASKED, jnp.int32)
    return jnp.concatenate([far, prev, own, future], axis=0)


def _moba_attn_kernel(qm_ref, km_ref, vm_ref, tb_ref, o_ref, m_ref, l_ref, acc_ref, *, cb):
    i = pl.program_id(2)
    c = pl.program_id(3)
    rows = MOBA_GROUP * MOBA_BLOCK
    scale = MOBA_DIM ** -0.5

    @pl.when(c == 0)
    def _():
        _softmax_init(m_ref, l_ref, acc_ref)

    @pl.when(c * cb <= i)
    def _():
        hh = MOBA_GROUP // ATTN_SPLIT
        nr = hh * MOBA_BLOCK
        kinds = [jnp.clip(2 - (i - (c * cb + p)), 0, MOBA_TILE_KINDS - 1) for p in range(cb)]
        ss = [_dot_nt(qm_ref[part * hh:(part + 1) * hh].reshape(nr, 2 * MOBA_DIM), km_ref[0]) * scale
              for part in range(ATTN_SPLIT)]
        for part in range(ATTN_SPLIT):
            s = ss[part].reshape(hh, MOBA_BLOCK, cb * MOBA_BLOCK)
            s = jnp.concatenate([s[:, :, p * MOBA_BLOCK:(p + 1) * MOBA_BLOCK]
                                 + tb_ref[kinds[p], part * hh:(part + 1) * hh] for p in range(cb)], axis=2)
            sl = slice(part * nr, (part + 1) * nr)
            _softmax_step(s.reshape(nr, cb * MOBA_BLOCK), m_ref.at[sl], l_ref.at[sl], acc_ref.at[sl],
                          lambda pr: _dot(pr, vm_ref[0]))

    @pl.when(c == pl.num_programs(3) - 1)
    def _():
        o = acc_ref[...] / l_ref[...]
        for hh in range(MOBA_GROUP):
            o_ref[:, hh * MOBA_DIM:(hh + 1) * MOBA_DIM] = o[hh * MOBA_BLOCK:(hh + 1) * MOBA_BLOCK].astype(BF16)


def _moba_prompt_attn(qm, km, vm, tb, batch, seq):
    nblk = seq // MOBA_BLOCK
    cb = min(MOBA_CHUNK_BLOCKS, nblk)
    nchunk = nblk // cb
    t = batch * seq
    rows = MOBA_GROUP * MOBA_BLOCK

    def kmap(b, g, i, c):
        return (g, b * nchunk + jnp.minimum(c, i // cb), 0)

    return pl.pallas_call(
        functools.partial(_moba_attn_kernel, cb=cb),
        grid=(batch, MOBA_KV_HEADS, nblk, nchunk),
        in_specs=[pl.BlockSpec((MOBA_GROUP, MOBA_BLOCK, 2 * MOBA_DIM), lambda b, g, i, c: (g, b * nblk + i, 0)),
                  pl.BlockSpec((1, cb * MOBA_BLOCK, 2 * MOBA_DIM), kmap),
                  pl.BlockSpec((1, cb * MOBA_BLOCK, MOBA_DIM), kmap),
                  pl.BlockSpec((MOBA_TILE_KINDS, MOBA_GROUP, MOBA_BLOCK, MOBA_BLOCK), lambda b, g, i, c: (0, g, 0, 0))],
        out_specs=pl.BlockSpec((MOBA_BLOCK, MOBA_GROUP * MOBA_DIM), lambda b, g, i, c: (b * nblk + i, g)),
        out_shape=jax.ShapeDtypeStruct((t, MOBA_HEADS * MOBA_DIM), BF16),
        scratch_shapes=[pltpu.VMEM((rows, 1), F32), pltpu.VMEM((rows, 1), F32),
                        pltpu.VMEM((rows, MOBA_DIM), F32)],
        compiler_params=_cparams(("parallel", "parallel", "parallel", "arbitrary")),
        name="moba_prompt_attn",
    )(qm, km, vm, tb)


def _merge_kernel(x_ref, oa_ref, ob_ref, za_ref, zb_ref, wa_ref, wb_ref, wo_ref, g_ref, x1_ref, h_ref):
    a = _dot(oa_ref[...], wa_ref[...])
    b = _dot(ob_ref[...], wb_ref[...])
    y = _sigmoid(za_ref[...]) * a + _sigmoid(zb_ref[...]) * b
    x1 = x_ref[...] + _dot(y.astype(BF16), wo_ref[...])
    x1_ref[...] = x1
    h_ref[...] = _rms(x1, g_ref[...]).astype(BF16)


def _merge(x, oa, ob, z, wa, wb, wo, g):
    t, d = x.shape
    tm = min(256, t)
    row = lambda c: (lambda i: (i, c))
    return pl.pallas_call(
        _merge_kernel,
        grid=(t // tm,),
        in_specs=[pl.BlockSpec((tm, d), row(0)),
                  pl.BlockSpec((tm, oa.shape[1]), row(0)),
                  pl.BlockSpec((tm, ob.shape[1]), row(0)),
                  pl.BlockSpec((tm, d), row(OFF_A // d)),
                  pl.BlockSpec((tm, d), row(OFF_B // d)),
                  _resident(wa.shape), _resident(wb.shape), _resident(wo.shape), _resident(g.shape)],
        out_specs=[pl.BlockSpec((tm, d), row(0)), pl.BlockSpec((tm, d), row(0))],
        out_shape=[jax.ShapeDtypeStruct((t, d), F32), jax.ShapeDtypeStruct((t, d), BF16)],
        compiler_params=_cparams(("parallel",)),
        name="mixer_merge",
    )(x, oa, ob, z, z, wa, wb, wo, g)


def _top_rows(s, k):
    row = lax.broadcasted_iota(jnp.int32, s.shape, 0)
    rest = s
    vals = []
    for _ in range(k):
        m = jnp.max(rest, axis=0, keepdims=True)
        idx = jnp.min(jnp.where(rest == m, row, s.shape[0]), axis=0, keepdims=True)
        rest = jnp.where(row == idx, -jnp.inf, rest)
        vals.append(m)
    return jnp.concatenate(vals, axis=0), jnp.where(rest == -jnp.inf, s, -jnp.inf)


def _peer_route_kernel(h_ref, wqt_ref, sk_ref, u0_ref, s1_ref, tau_ref, invz_ref):
    qt = _dot_nt(wqt_ref[...], h_ref[...])
    nk = PEER_NKEYS
    for p in range(PEER_HEADS):
        halves = []
        for half in range(2):
            g = 2 * p + half
            s = _dot(sk_ref[g], qt[g * nk:(g + 1) * nk].astype(BF16))
            halves.append(_top_rows(s, PEER_TOPK))
        (v0, s0), (v1, s1) = halves
        top = v0[0:1] + v1[0:1]
        u0 = s0 - top
        v0s = v0 - top
        hk = PEER_TOPK // 2
        cand = jnp.concatenate([v0s[a:a + 1] + v1[:hk] for a in range(hk)]
                               + [v0s[0:1] + v1[hk:], v0s[hk:] + v1[0:1]], axis=0)
        tau = _top_rows(cand, PEER_TOPK)[0][PEER_TOPK - 1:PEER_TOPK]
        z = jnp.sum(jnp.where(cand >= tau, jnp.exp(cand), 0.0), axis=0, keepdims=True)
        u0_ref[p] = u0
        s1_ref[p] = s1
        tau_ref[p:p + 1, :] = tau
        invz_ref[p:p + 1, :] = 1.0 / z


def _peer_route(h, wqt, sk):
    t, d = h.shape
    tt = min(256, t)
    nk = PEER_NKEYS
    tok = lambda i: (0, 0, i)
    return pl.pallas_call(
        _peer_route_kernel,
        grid=(t // tt,),
        in_specs=[pl.BlockSpec((tt, d), lambda i: (i, 0)), _resident(wqt.shape), _resident(sk.shape)],
        out_specs=[pl.BlockSpec((PEER_HEADS, nk, tt), tok), pl.BlockSpec((PEER_HEADS, nk, tt), tok),
                   pl.BlockSpec((PEER_HEADS, tt), lambda i: (0, i)),
                   pl.BlockSpec((PEER_HEADS, tt), lambda i: (0, i))],
        out_shape=[jax.ShapeDtypeStruct((PEER_HEADS, nk, t), F32), jax.ShapeDtypeStruct((PEER_HEADS, nk, t), F32),
                   jax.ShapeDtypeStruct((PEER_HEADS, t), F32), jax.ShapeDtypeStruct((PEER_HEADS, t), F32)],
        compiler_params=_cparams(("parallel",)),
        name="peer_route",
    )(h, wqt, sk)


PEER_EXPERT_TILE = 1024
PEER_ROWS_PER_TILE = PEER_EXPERT_TILE // PEER_NKEYS
PEER_SUB_ROWS = 2


def _peer_dense_kernel(h_ref, u_ref, vt_ref, u0_ref, s1_ref, tau_ref, invz_ref, o_ref, acc_ref):
    e = pl.program_id(1)

    @pl.when(e == 0)
    def _():
        acc_ref[...] = jnp.zeros(acc_ref.shape, F32)

    nk = PEER_NKEYS
    sub = PEER_SUB_ROWS * nk
    nsub = PEER_EXPERT_TILE // sub
    h = h_ref[...]
    half = PEER_EXPERT_TILE // 2
    halves = [_dot_nt(u_ref[k * half:(k + 1) * half, :], h) for k in range(2)]
    per_half = half // sub
    out = None
    for k in range(nsub):
        at = halves[k // per_half][(k % per_half) * sub:(k % per_half + 1) * sub]
        act = 0.5 * at * (1.0 + lax.erf(at * (2.0 ** -0.5)))
        ws = []
        for r in range(PEER_SUB_ROWS):
            ii = k * PEER_SUB_ROWS + r
            gate = jnp.zeros((nk, at.shape[1]), F32)
            for p in range(PEER_HEADS):
                val = u0_ref[p, ii:ii + 1, :] + s1_ref[p]
                gate = gate + jnp.where(val >= tau_ref[p:p + 1, :], jnp.exp(val) * invz_ref[p:p + 1, :], 0.0)
            ws.append((gate * act[r * nk:(r + 1) * nk]).astype(BF16))
        part = _dot(vt_ref[:, k * sub:(k + 1) * sub], jnp.concatenate(ws, axis=0))
        out = part if out is None else out + part
    acc_ref[...] += out

    @pl.when(e == pl.num_programs(1) - 1)
    def _():
        o_ref[...] = acc_ref[...].T


def _peer_dense(h, u, vt, u0, s1, tau, invz):
    t, d = h.shape
    tt = min(512, t)
    te = PEER_EXPERT_TILE
    n_exp = u.shape[0]
    nk = PEER_NKEYS
    return pl.pallas_call(
        _peer_dense_kernel,
        grid=(t // tt, n_exp // te),
        in_specs=[pl.BlockSpec((tt, d), lambda i, e: (i, 0)),
                  pl.BlockSpec((te, d), lambda i, e: (e, 0)),
                  pl.BlockSpec((d, te), lambda i, e: (0, e)),
                  pl.BlockSpec((PEER_HEADS, PEER_ROWS_PER_TILE, tt), lambda i, e: (0, e, i)),
                  pl.BlockSpec((PEER_HEADS, nk, tt), lambda i, e: (0, 0, i)),
                  pl.BlockSpec((PEER_HEADS, tt), lambda i, e: (0, i)),
                  pl.BlockSpec((PEER_HEADS, tt), lambda i, e: (0, i))],
        out_specs=pl.BlockSpec((tt, d), lambda i, e: (i, 0)),
        out_shape=jax.ShapeDtypeStruct((t, d), F32),
        scratch_shapes=[pltpu.VMEM((d, tt), F32)],
        compiler_params=_cparams(("parallel", "arbitrary")),
        name="peer_dense",
    )(h, u, vt, u0, s1, tau, invz)


def _ple_final_kernel(x_ref, f_ref, p_ref, gp_ref, wg_ref, wp_ref, gf_ref, o_ref):
    x = x_ref[...] + f_ref[...]
    gate = _sigmoid(_dot(_rms(x, gp_ref[...]).astype(BF16), wg_ref[...]))
    x = x + gate * _dot(p_ref[...].astype(BF16), wp_ref[...])
    o_ref[...] = _rms(x, gf_ref[...])


def _ple_final(x, ffn, p, gp, wg, wp, gf):
    t, d = x.shape
    tm = min(256, t)
    return pl.pallas_call(
        _ple_final_kernel,
        grid=(t // tm,),
        in_specs=[pl.BlockSpec((tm, d), lambda i: (i, 0)),
                  pl.BlockSpec((tm, d), lambda i: (i, 0)),
                  pl.BlockSpec((tm, p.shape[1]), lambda i: (i, 0)),
                  _resident(gp.shape), _resident(wg.shape), _resident(wp.shape), _resident(gf.shape)],
        out_specs=pl.BlockSpec((tm, d), lambda i: (i, 0)),
        out_shape=jax.ShapeDtypeStruct((t, d), F32),
        compiler_params=_cparams(("parallel",)),
        name="ple_final",
    )(x, ffn, p, gp, wg, wp, gf)


NEW_PAD = LANES
SAMPLE_PAGES_PER_STEP = 32


def _page_copies(pt_ref, b, j, slot, pg, streams):
    copies = []
    for k in range(pg):
        page = pt_ref[b, j * pg + k]
        for hbm, buf, sem in streams:
            copies.append(pltpu.make_async_copy(hbm.at[0, page], buf.at[slot, k], sem.at[slot]))
    return copies


def _paged_prefetch(pt_ref, pg, streams):
    b, j = pl.program_id(0), pl.program_id(1)
    nb, nj = pl.num_programs(0), pl.num_programs(1)
    g = b * nj + j
    slot = g % 2

    @pl.when(g == 0)
    def _():
        for c in _page_copies(pt_ref, b, j, slot, pg, streams):
            c.start()

    @pl.when(g + 1 < nb * nj)
    def _():
        wrap = j + 1 == nj
        for c in _page_copies(pt_ref, jnp.where(wrap, b + 1, b), jnp.where(wrap, 0, j + 1), 1 - slot, pg, streams):
            c.start()

    for c in _page_copies(pt_ref, b, j, slot, pg, streams):
        c.wait()
    return slot


def _mla_sample_kernel(pt_ref, qa_ref, qr_ref, kcn_ref, krn_ref, ckv_hbm, kr_hbm, o_ref, m_ref, l_ref, acc_ref,
                       cbuf, rbuf, csem, rsem, *, ds, pg):
    j = pl.program_id(1)
    scale = (MLA_NOPE + MLA_ROPE) ** -0.5
    slot = _paged_prefetch(pt_ref, pg, [(ckv_hbm, cbuf, csem), (kr_hbm, rbuf, rsem)])

    @pl.when(j == 0)
    def _():
        _softmax_init(m_ref, l_ref, acc_ref)

    qa = qa_ref[...]
    qr = qr_ref[...]
    cs = [cbuf[slot, k].astype(BF16) for k in range(pg)]
    s = jnp.concatenate([_dot_nt(qa, cs[k]) + _dot(qr, rbuf[slot, k].astype(BF16)) for k in range(pg)],
                        axis=1) * scale

    def pv(p):
        out = _dot(p[:, :PAGE], cs[0])
        for k in range(1, pg):
            out = out + _dot(p[:, k * PAGE:(k + 1) * PAGE], cs[k])
        return out

    _softmax_step(s, m_ref, l_ref, acc_ref, pv)

    @pl.when(j == pl.num_programs(1) - 1)
    def _():
        kcn = kcn_ref[...]
        sn = (_dot_nt(qa, kcn) + _dot_nt(qr, krn_ref[...])) * scale
        tok = lax.broadcasted_iota(jnp.int32, sn.shape, 0) & (ds - 1)
        col = lax.broadcasted_iota(jnp.int32, sn.shape, 1)
        sn = jnp.where(col <= tok, sn, NEG_INF)
        _softmax_step(sn, m_ref, l_ref, acc_ref, lambda p: _dot(p, kcn))
        o_ref[...] = acc_ref[...] / l_ref[...]


def _mla_sample_attn(page_table, qa, qr, kcn, krn, cache_ckv, cache_krope_t, ds):
    nseq, rows, _ = qa.shape
    pg = min(SAMPLE_PAGES_PER_STEP, page_table.shape[1])
    nsteps = page_table.shape[1] // pg
    seq3 = lambda b, j, pt: (b, 0, 0)
    grid_spec = pltpu.PrefetchScalarGridSpec(
        num_scalar_prefetch=1,
        grid=(nseq, nsteps),
        in_specs=[pl.BlockSpec((None, rows, MLA_LORA), seq3),
                  pl.BlockSpec((None, rows, MLA_ROPE), seq3),
                  pl.BlockSpec((None, NEW_PAD, MLA_LORA), seq3),
                  pl.BlockSpec((None, NEW_PAD, MLA_ROPE), seq3),
                  pl.BlockSpec(memory_space=pl.ANY),
                  pl.BlockSpec(memory_space=pl.ANY)],
        out_specs=pl.BlockSpec((None, rows, MLA_LORA), seq3),
        scratch_shapes=[pltpu.VMEM((rows, 1), F32), pltpu.VMEM((rows, 1), F32), pltpu.VMEM((rows, MLA_LORA), F32),
                        pltpu.VMEM((2, pg, PAGE, MLA_LORA), F32), pltpu.VMEM((2, pg, MLA_ROPE, PAGE), F32),
                        pltpu.SemaphoreType.DMA((2,)), pltpu.SemaphoreType.DMA((2,))],
    )
    return pl.pallas_call(
        functools.partial(_mla_sample_kernel, ds=ds, pg=pg),
        grid_spec=grid_spec,
        out_shape=jax.ShapeDtypeStruct((nseq, rows, MLA_LORA), F32),
        compiler_params=_cparams(("arbitrary", "arbitrary")),
        name="mla_sample_attn",
    )(page_table, qa, qr, kcn, krn, cache_ckv, cache_krope_t)


def _head_proj_kernel(o_ref, w_ref, out_ref):
    for h in range(MLA_HEADS):
        out_ref[:, h * MLA_V:(h + 1) * MLA_V] = _dot(o_ref[h], w_ref[h]).astype(BF16)


def _head_proj(o_lat, wuv):
    t = o_lat.shape[1]
    return pl.pallas_call(
        _head_proj_kernel,
        out_shape=jax.ShapeDtypeStruct((t, MLA_HEADS * MLA_V), BF16),
        name="mla_head_proj",
    )(o_lat, wuv)


def _moba_sample_kernel(pt_ref, q_ref, bias_ref, kn_ref, vn_ref, bown_ref, k_hbm, v_hbm, o_ref, mean_ref, m_ref,
                        l_ref, part_ref, kbuf, vbuf, ksem, vsem, *, ds, pg, nblk):
    slot = _paged_prefetch(pt_ref, pg, [(k_hbm, kbuf, ksem), (v_hbm, vbuf, vsem)])
    j = pl.program_id(1)
    last = pl.num_programs(1) - 1
    bps = pg // 2
    scale = MOBA_DIM ** -0.5
    rows = q_ref.shape[0]
    rows_g = rows // MOBA_KV_HEADS

    @pl.when(j == 0)
    def _():
        mean_ref[...] = jnp.zeros(mean_ref.shape, F32)
        m_ref[...] = jnp.zeros(m_ref.shape, F32)
        l_ref[...] = jnp.zeros(l_ref.shape, F32)

    q = q_ref[...]
    lane = lax.broadcasted_iota(jnp.int32, (rows, LANES), 1)
    sub_head = lax.broadcasted_iota(jnp.int32, (8, MOBA_DIM), 0) % MOBA_KV_HEADS
    m_all = m_ref[...]
    l_all = l_ref[...]
    means, scores = [], []
    for n in range(bps):
        ka = kbuf[slot, 2 * n]
        kb = kbuf[slot, 2 * n + 1]
        by_sublane = (ka.reshape(-1, 8, MOBA_DIM).sum(axis=0) + kb.reshape(-1, 8, MOBA_DIM).sum(axis=0))
        means.append(jnp.concatenate(
            [jnp.sum(jnp.where(sub_head == g, by_sublane, 0.0), axis=0, keepdims=True) for g in range(MOBA_KV_HEADS)],
            axis=1))
        kk = jnp.concatenate([ka, kb], axis=0).astype(BF16)
        tile = bias_ref[jnp.where(j == last, 1, 0)] if n == bps - 1 else bias_ref[0]
        scores.append(_dot_nt(q, kk) * scale + tile)
    probs = []
    for n in range(bps):
        mb = jnp.max(scores[n], axis=-1, keepdims=True)
        p = jnp.exp(scores[n] - mb)
        hit = lane == j * bps + n
        m_all = jnp.where(hit, mb, m_all)
        l_all = jnp.where(hit, jnp.sum(p, axis=-1, keepdims=True), l_all)
        probs.append(p.astype(BF16))
    for n in range(bps):
        vv = jnp.concatenate([vbuf[slot, 2 * n], vbuf[slot, 2 * n + 1]], axis=0).astype(BF16)
        part_ref[j * bps + n] = _dot(probs[n], vv)
    m_ref[...] = m_all
    l_ref[...] = l_all
    mean_ref[pl.ds(pl.multiple_of(j * bps, bps), bps), :] = jnp.concatenate(means, axis=0) * (1.0 / MOBA_BLOCK)

    @pl.when(j == last)
    def _():
        km = mean_ref[...].astype(BF16)

        def group(x, g):
            return x[:, g * MOBA_DIM:(g + 1) * MOBA_DIM]

        def per_group(fn):
            return jnp.concatenate([fn(g, q[g * rows_g:(g + 1) * rows_g]) for g in range(MOBA_KV_HEADS)], axis=0)

        past = lane < nblk
        block_scores = per_group(lambda g, qg: _dot_nt(qg, group(km, g)))
        sel = jnp.logical_and(_top_mask(jnp.where(past, block_scores, NEG_INF), min(MOBA_TOPK, nblk)), past)

        kn = kn_ref[...]
        vn = vn_ref[...]
        sn = per_group(lambda g, qg: _dot_nt(qg, group(kn, g))) * scale + bown_ref[...]
        tok = lax.broadcasted_iota(jnp.int32, sn.shape, 0) & (ds - 1)
        col = lax.broadcasted_iota(jnp.int32, sn.shape, 1)
        sn = jnp.where(col <= tok, sn, NEG_INF)
        m_own = jnp.max(sn, axis=-1, keepdims=True)
        p_own = jnp.exp(sn - m_own)
        l_own = jnp.sum(p_own, axis=-1, keepdims=True)
        pb = p_own.astype(BF16)
        o_own = jnp.concatenate([_dot(pb[g * rows_g:(g + 1) * rows_g], group(vn, g)) for g in range(MOBA_KV_HEADS)],
                                axis=0)

        m_top = jnp.maximum(jnp.max(jnp.where(sel, m_all, -jnp.inf), axis=-1, keepdims=True), m_own)
        w = jnp.where(sel, jnp.exp(m_all - m_top), 0.0)
        w_own = jnp.exp(m_own - m_top)
        den = jnp.sum(w * l_all, axis=-1, keepdims=True) + w_own * l_own
        num = w_own * o_own
        for b in range(nblk):
            num = num + w[:, b:b + 1] * part_ref[b]
        o_ref[...] = num / den


def _moba_sample_attn(page_table, q, bias, kn, vn, bown, cache_k, cache_v, ds):
    nseq, rows, _ = q.shape
    npages = page_table.shape[1]
    pg = min(SAMPLE_PAGES_PER_STEP, npages)
    nsteps = npages // pg
    nblk = npages * PAGE // MOBA_BLOCK
    kvw = MOBA_KV_HEADS * MOBA_DIM
    page_rows = PAGE * MOBA_KV_HEADS
    block_cols = MOBA_BLOCK * MOBA_KV_HEADS
    seq3 = lambda b, j, pt: (b, 0, 0)
    grid_spec = pltpu.PrefetchScalarGridSpec(
        num_scalar_prefetch=1,
        grid=(nseq, nsteps),
        in_specs=[pl.BlockSpec((None, rows, MOBA_DIM), seq3),
                  pl.BlockSpec((2, rows, block_cols), lambda b, j, pt: (0, 0, 0)),
                  pl.BlockSpec((None, NEW_PAD, kvw), seq3),
                  pl.BlockSpec((None, NEW_PAD, kvw), seq3),
                  pl.BlockSpec((rows, NEW_PAD), lambda b, j, pt: (0, 0)),
                  pl.BlockSpec(memory_space=pl.ANY),
                  pl.BlockSpec(memory_space=pl.ANY)],
        out_specs=pl.BlockSpec((None, rows, MOBA_DIM), seq3),
        scratch_shapes=[pltpu.VMEM((LANES, kvw), F32), pltpu.VMEM((rows, LANES), F32), pltpu.VMEM((rows, LANES), F32),
                        pltpu.VMEM((nblk, rows, MOBA_DIM), F32),
                        pltpu.VMEM((2, pg, page_rows, MOBA_DIM), F32), pltpu.VMEM((2, pg, page_rows, MOBA_DIM), F32),
                        pltpu.SemaphoreType.DMA((2,)), pltpu.SemaphoreType.DMA((2,))],
    )
    return pl.pallas_call(
        functools.partial(_moba_sample_kernel, ds=ds, pg=pg, nblk=nblk),
        grid_spec=grid_spec,
        out_shape=jax.ShapeDtypeStruct((nseq, rows, MOBA_DIM), F32),
        compiler_params=_cparams(("arbitrary", "arbitrary")),
        name="moba_sample_attn",
    )(page_table, q, bias, kn, vn, bown, cache_k, cache_v)


def _rope_tables(pos):
    inv = jnp.exp(-math.log(ROPE_THETA) * jnp.arange(0, MLA_ROPE, 2, dtype=F32) / MLA_ROPE)
    ang = pos.astype(F32)[:, None] * inv[None, :]
    return jnp.tile(jnp.cos(ang), (1, MLA_HEADS)), jnp.tile(jnp.sin(ang), (1, MLA_HEADS))


def _t5_bucket(dist):
    dist = jnp.maximum(dist, 0)
    max_exact = REL_BUCKETS // 2
    scaled = (jnp.log(jnp.maximum(dist, 1).astype(F32) / max_exact)
              / math.log(REL_MAX_DIST / max_exact) * (REL_BUCKETS - max_exact))
    large = jnp.minimum(max_exact + scaled.astype(jnp.int32), REL_BUCKETS - 1)
    return jnp.where(dist < max_exact, dist, large)


def _rope_select_matrices():
    half = MLA_ROPE // 2
    sela = np.zeros((MLA_HEADS * half, MLA_HEADS * LANES), np.float32)
    selb = np.zeros_like(sela)
    for h in range(MLA_HEADS):
        for r in range(half):
            sela[h * half + r, h * LANES + r] = 1.0
            selb[h * half + r, h * LANES + half + r] = 1.0
    return jnp.asarray(sela, BF16), jnp.asarray(selb, BF16)


def _prepare_weights(w_in, w_uq, w_uk, w_uv, w_a_out, w_b_out, w_o, peer_wq, peer_subkeys, peer_u, peer_v,
                     w_ple_gate, w_ple_proj):
    d = D_MODEL
    lo = MLA_LORA
    o_q, o_c, o_r = 0, lo, 2 * lo
    o_mq = o_r + MLA_ROPE
    o_mk = o_mq + MOBA_HEADS * MOBA_DIM
    o_mv = o_mk + MOBA_KV_HEADS * MOBA_DIM
    o_a = o_mv + MOBA_KV_HEADS * MOBA_DIM
    o_b = o_a + d
    cols = [w_in[:, o_a:o_b], w_in[:, o_b:o_b + d], w_in[:, o_q:o_c], w_in[:, o_c:o_r], w_in[:, o_mq:o_mk],
            w_in[:, o_mk:o_mv], w_in[:, o_mv:o_a], w_in[:, o_r:o_mq]]
    used = sum(c.shape[1] for c in cols)
    w_in_p = jnp.concatenate(cols + [jnp.zeros((d, Z_WIDTH - used), w_in.dtype)], axis=1).astype(BF16)

    hd = MLA_NOPE + MLA_ROPE
    half = MLA_ROPE // 2
    uq = w_uq.reshape(lo, MLA_HEADS, hd)
    w_uq_p = jnp.concatenate([uq[:, :, :MLA_NOPE].reshape(lo, -1), uq[:, :, MLA_NOPE:MLA_NOPE + half].reshape(lo, -1),
                              uq[:, :, MLA_NOPE + half:].reshape(lo, -1)], axis=1).astype(BF16)
    sela, selb = _rope_select_matrices()
    return dict(
        w_in=w_in_p, w_uq=w_uq_p, sela=sela, selb=selb,
        w_ukt=jnp.transpose(w_uk, (1, 2, 0)).astype(BF16),
        w_uv=jnp.transpose(w_uv, (1, 0, 2)).astype(BF16),
        w_a=w_a_out.astype(BF16), w_b=w_b_out.astype(BF16), w_o=w_o.astype(BF16),
        wqt=peer_wq.T.astype(BF16),
        sk=peer_subkeys.reshape(2 * PEER_HEADS, PEER_NKEYS, -1).astype(BF16),
        u=peer_u.astype(BF16), vt=peer_v.T.astype(BF16),
        w_gate=w_ple_gate.astype(BF16), w_proj=w_ple_proj.astype(BF16),
    )


def _row(v):
    return v.reshape(1, -1)


def _channel_tail(x1, h2, p, w, g_ple, g_final):
    u0, s1, tau, invz = _peer_route(h2, w["wqt"], w["sk"])
    ffn = _peer_dense(h2, w["u"], w["vt"], u0, s1, tau, invz)
    return _ple_final(x1, ffn, p, _row(g_ple), w["w_gate"], w["w_proj"], _row(g_final))


def _prompt_group(x, p, rel_bias, w, g_mix, g_q_lat, g_kv_lat, g_ffn, g_ple, g_final):
    batch, seq, d = x.shape
    t = batch * seq
    xf = x.reshape(t, d)
    z = _norm_matmul(xf, _row(g_mix), w["w_in"], Z_TILE)
    cos, sin = _rope_tables(jnp.tile(jnp.arange(seq), batch))
    ckv, krope, kc, kr, qa, qr = _mla_prep(z, cos, sin, _row(g_q_lat), w["w_uq"], _row(g_kv_lat), w["w_ukt"],
                                           w["sela"], w["selb"])
    oa = _mla_prompt_attn(qa, qr, kc, kr, w["w_uv"], batch, seq)

    nblk = seq // MOBA_BLOCK
    kmean, km, vm = _moba_kprep(z, seq)
    kmean = jnp.pad(kmean.reshape(batch, nblk, -1), ((0, 0), (0, LANES - nblk), (0, 0)))
    qm = _moba_select(z, kmean, seq)
    tb = _bias_table(rel_bias, _moba_tile_buckets())
    tb = jnp.transpose(tb.reshape(MOBA_HEADS, MOBA_TILE_KINDS, MOBA_BLOCK, MOBA_BLOCK), (1, 0, 2, 3))
    ob = _moba_prompt_attn(qm, km, vm, tb, batch, seq)

    x1, h2 = _merge(xf, oa, ob, z, w["w_a"], w["w_b"], w["w_o"], _row(g_ffn))
    y = _channel_tail(x1, h2, p.reshape(t, -1), w, g_ple, g_final)
    mk = z[:, OFF_MK:OFF_MV].reshape(batch, seq, MOBA_KV_HEADS, MOBA_DIM)
    mv = z[:, OFF_MV:OFF_R].reshape(batch, seq, MOBA_KV_HEADS, MOBA_DIM)
    return (y.reshape(batch, seq, d), ckv.reshape(batch, seq, -1), krope.reshape(batch, seq, -1), mk, mv)


def _head_major(a, nseq, ds):
    h, _, width = a.shape
    return jnp.transpose(a.reshape(h, nseq, ds, width), (1, 0, 2, 3)).reshape(nseq, h * ds, width)


def _pad_new(a, nseq, ds):
    return jnp.pad(a.reshape(nseq, ds, -1), ((0, 0), (0, NEW_PAD - ds), (0, 0)))


def _sample_group(x, p, page_table, caches, rel_bias, w, g_mix, g_q_lat, g_kv_lat, g_ffn, g_ple, g_final):
    cache_ckv, cache_krope, cache_k, cache_v = caches
    nseq, ds, d = x.shape
    t = nseq * ds
    npages = page_table.shape[1]
    past = npages * PAGE
    xf = x.reshape(t, d)
    z = _norm_matmul(xf, _row(g_mix), w["w_in"], Z_TILE)
    cos, sin = _rope_tables(jnp.tile(past + jnp.arange(ds), nseq))
    ckv, krope, kc, kr, qa, qr = _mla_prep(z, cos, sin, _row(g_q_lat), w["w_uq"], _row(g_kv_lat), w["w_ukt"],
                                           w["sela"], w["selb"])
    krope_t = jnp.transpose(cache_krope, (0, 1, 3, 2))
    o_lat = _mla_sample_attn(page_table, _head_major(qa, nseq, ds), _head_major(qr[:, :, :MLA_ROPE], nseq, ds),
                             _pad_new(kc, nseq, ds), _pad_new(kr[:, :MLA_ROPE], nseq, ds),
                             cache_ckv, krope_t, ds)
    o_lat = jnp.transpose(o_lat.reshape(nseq, MLA_HEADS, ds, -1), (1, 0, 2, 3)).reshape(MLA_HEADS, t, -1)
    oa = _head_proj(o_lat.astype(BF16), w["w_uv"])

    n_pool = cache_k.shape[1]
    ck = cache_k.reshape(1, n_pool, PAGE * MOBA_KV_HEADS, MOBA_DIM)
    cv = cache_v.reshape(1, n_pool, PAGE * MOBA_KV_HEADS, MOBA_DIM)
    mq = z[:, OFF_MQ:OFF_MK].astype(BF16).reshape(t, MOBA_HEADS, MOBA_DIM)
    mq = _head_major(jnp.transpose(mq, (1, 0, 2)), nseq, ds)

    col = jnp.arange(MOBA_BLOCK * MOBA_KV_HEADS)[None, :]
    dist_last = (MOBA_BLOCK + jnp.arange(ds)[:, None]) - col // MOBA_KV_HEADS
    rows_b = []
    for g in range(MOBA_KV_HEADS):
        mine = col % MOBA_KV_HEADS == g
        rows_b.append(jnp.where(mine, REL_BUCKETS - 1, BIAS_MASKED) + jnp.zeros((ds, 1), jnp.int32))
        rows_b.append(jnp.where(mine, _t5_bucket(dist_last), BIAS_MASKED))
    bias = _bias_table(rel_bias, jnp.concatenate(rows_b, axis=0))
    bias = bias.reshape(MOBA_HEADS, MOBA_KV_HEADS, 2, ds, -1)
    heads = np.arange(MOBA_HEADS)
    bias = jnp.transpose(bias[heads, heads // MOBA_GROUP], (1, 0, 2, 3)).reshape(2, MOBA_HEADS * ds, -1)
    own = _t5_bucket(jnp.arange(ds)[:, None] - jnp.arange(NEW_PAD)[None, :])
    bown = _bias_table(rel_bias, own).reshape(MOBA_HEADS * ds, NEW_PAD)
    kn = _pad_new(z[:, OFF_MK:OFF_MV].astype(BF16), nseq, ds)
    vn = _pad_new(z[:, OFF_MV:OFF_R].astype(BF16), nseq, ds)
    o_m = _moba_sample_attn(page_table, mq, bias, kn, vn, bown, ck, cv, ds)
    ob = jnp.transpose(o_m.reshape(nseq, MOBA_HEADS, ds, MOBA_DIM), (0, 2, 1, 3)).reshape(t, -1).astype(BF16)

    x1, h2 = _merge(xf, oa, ob, z, w["w_a"], w["w_b"], w["w_o"], _row(g_ffn))
    y = _channel_tail(x1, h2, p.reshape(t, -1), w, g_ple, g_final)
    mk = z[:, OFF_MK:OFF_MV].reshape(nseq, ds, MOBA_KV_HEADS, MOBA_DIM)
    mv = z[:, OFF_MV:OFF_R].reshape(nseq, ds, MOBA_KV_HEADS, MOBA_DIM)
    return (y.reshape(nseq, ds, d), ckv.reshape(nseq, ds, -1), krope.reshape(nseq, ds, -1), mk, mv)


def kernel(x_prompt, x_sample, cache_mla_ckv, cache_mla_krope, cache_moba_k, cache_moba_v, page_table,
           p_prompt, p_sample, rel_bias, g_mix, w_in, g_q_lat, w_uq, g_kv_lat, w_uk, w_uv, w_a_out,
           w_b_out, w_o, g_ffn, peer_wq, peer_subkeys, peer_u, peer_v, g_ple, w_ple_gate, w_ple_proj, g_final):
    assert g_mix.shape[0] == 1, "single-layer step"
    w = _prepare_weights(w_in[0], w_uq[0], w_uk[0], w_uv[0], w_a_out[0], w_b_out[0], w_o[0], peer_wq[0],
                         peer_subkeys[0], peer_u[0], peer_v[0], w_ple_gate[0], w_ple_proj[0])
    gains = (g_mix[0], g_q_lat[0], g_kv_lat[0], g_ffn[0], g_ple[0], g_final)
    yp, ckv_p, kr_p, k_p, v_p = _prompt_group(x_prompt, p_prompt[0], rel_bias, w, *gains)
    caches = (cache_mla_ckv, cache_mla_krope, cache_moba_k, cache_moba_v)
    ys, ckv_s, kr_s, k_s, v_s = _sample_group(x_sample, p_sample[0], page_table, caches, rel_bias, w, *gains)
    return (yp, ys, ckv_p[None], kr_p[None], k_p[None], v_p[None], ckv_s[None], kr_s[None], k_s[None], v_s[None])
```
